```python
import jax, jax.numpy as jnp
from jax import lax
import numpy as np


D_MODEL = 4096
BATCH = 2
SEQ = 8192
DEPTH = 2
DEC_BATCH = 16
DEC_SEQ = 16
PAST_LEN = 4096

CHUNK = 64
N_MIXERS = 2
N_POOL_LAYERS = (DEPTH + 1) // 2
N_ATTN_LAYERS = DEPTH // 2
POOL_WINDOWS = (2, 4, 8, 16)
N_POOL_GROUPS = 4
POOL_GROUP_DIM = D_MODEL // N_POOL_GROUPS
POOL_STATE = 15
HEAD_DIM = 64
N_HEADS = D_MODEL // HEAD_DIM
N_LEFT_CHUNKS = 8
BAND_PAST = N_LEFT_CHUNKS * CHUNK
BAND = BAND_PAST + CHUNK
REL_CLIP = 256
N_GROUPS = 4
EXPERTS_PER_GROUP = 8
N_EXPERTS = N_GROUPS * EXPERTS_PER_GROUP
TOP_K_INNER = 2
D_EXPERT = 512
DEEPNORM_ALPHA = (2 * DEPTH) ** 0.25
DEEPNORM_BETA = (8 * DEPTH) ** -0.25
LN_EPS = 1e-5
NEG_INF = -1e30

kernel_name = 'hybrid_pool_chunkattn_hmoe_stream_step'


def layer_norm(x, g, b):
    xf = x.astype(jnp.float32)
    mu = jnp.mean(xf, -1, keepdims=True)
    var = jnp.mean(jnp.square(xf - mu), -1, keepdims=True)
    return ((xf - mu) * lax.rsqrt(var + LN_EPS) * g.astype(jnp.float32) + b.astype(jnp.float32)).astype(x.dtype)


def adaln_params(c, w, b):
    m = jax.nn.silu(c) @ w + b
    return jnp.split(m[:, None, :], 6, axis=-1)


def modulate(x, shift, scale):
    return x * (1 + scale) + shift


def post_norm(x, sub, gate, g, b):
    return layer_norm(DEEPNORM_ALPHA * x + (1 + gate) * sub, g, b)


def pool_mixer(h, hist, pos0, w_pool, ls):
    B, T, D = h.shape
    P = POOL_STATE
    hh = jnp.concatenate([hist.astype(h.dtype), h], axis=1)
    cs = jnp.pad(jnp.cumsum(hh.astype(jnp.float32), axis=1), ((0, 0), (1, 0), (0, 0)))
    pos = pos0 + jnp.arange(T)
    outs = []
    for g, w in enumerate(POOL_WINDOWS):
        sl = slice(g * POOL_GROUP_DIM, (g + 1) * POOL_GROUP_DIM)
        wsum = cs[:, P + 1:P + T + 1, sl] - cs[:, P + 1 - w:P + T + 1 - w, sl]
        cnt = jnp.minimum(pos + 1, w).astype(jnp.float32)[None, :, None]
        outs.append(wsum / cnt - h[..., sl].astype(jnp.float32))
    pooled = jnp.stack(outs, axis=2).astype(h.dtype)
    y = jnp.einsum('btgc,gce->btge', pooled, w_pool).reshape(B, T, D) * ls
    return y, hh[:, -P:]


def qkv_split(h, w_qkv):
    B, T, _ = h.shape
    qkv = (h @ w_qkv).reshape(B, T, 3, N_HEADS, HEAD_DIM)
    return qkv[:, :, 0], qkv[:, :, 1], qkv[:, :, 2]


def band_attend(q, k, v, qpos, kpos, rel_bias):
    s = jnp.einsum('bqhd,bkhd->bhqk', q, k).astype(jnp.float32) * (HEAD_DIM ** -0.5)
    rel = jnp.clip(qpos[:, None] - kpos[None, :], -REL_CLIP, REL_CLIP) + REL_CLIP
    s = s + rel_bias[:, rel].astype(jnp.float32)[None]
    qc = qpos // CHUNK
    kc = kpos // CHUNK
    mask = (kpos[None, :] >= 0) & (kc[None, :] <= qc[:, None]) & (kc[None, :] >= qc[:, None] - N_LEFT_CHUNKS)
    p = jax.nn.softmax(jnp.where(mask[None, None], s, NEG_INF), axis=-1)
    return jnp.einsum('bhqk,bkhd->bqhd', p.astype(v.dtype), v)


def attn_prompt(h, w_qkv, w_o, rel_bias):
    B, T, D = h.shape
    q, k, v = qkv_split(h, w_qkv)
    nc = T // CHUNK
    pad = ((0, 0), (BAND_PAST, 0), (0, 0), (0, 0))
    kpad, vpad = jnp.pad(k, pad), jnp.pad(v, pad)
    qch = jnp.moveaxis(q.reshape(B, nc, CHUNK, N_HEADS, HEAD_DIM), 1, 0)

    def one_chunk(args):
        ci, qb = args
        start = ci * CHUNK
        kb = lax.dynamic_slice_in_dim(kpad, start, BAND, axis=1)
        vb = lax.dynamic_slice_in_dim(vpad, start, BAND, axis=1)
        qpos = start + jnp.arange(CHUNK)
        kpos = start - BAND_PAST + jnp.arange(BAND)
        return band_attend(qb, kb, vb, qpos, kpos, rel_bias)

    o = lax.map(one_chunk, (jnp.arange(nc), qch))
    o = jnp.moveaxis(o, 0, 1).reshape(B, T, D)
    keep = min(BAND_PAST, T)
    return o @ w_o, k[:, T - keep:], v[:, T - keep:]


def attn_sample(h, ck, cv, w_qkv, w_o, rel_bias):
    B, T, D = h.shape
    q, k, v = qkv_split(h, w_qkv)
    rows = ck.shape[1]
    kall = jnp.concatenate([ck.astype(k.dtype), k], axis=1)
    vall = jnp.concatenate([cv.astype(v.dtype), v], axis=1)
    qpos = PAST_LEN + jnp.arange(T)
    kpos = PAST_LEN - rows + jnp.arange(rows + T)
    o = band_attend(q, kall, vall, qpos, kpos, rel_bias).reshape(B, T, D)
    return o @ w_o, k, v


def hier_moe(h, rg_w, rg_b, re_w, re_b, w_gate, w_up, w_down):
    B, T, D = h.shape
    t = h.reshape(B * T, D)
    tf = t.astype(jnp.float32)
    g_logits = tf @ rg_w.astype(jnp.float32) + rg_b.astype(jnp.float32)
    g_prob = jax.nn.softmax(g_logits, axis=-1)
    _, g_idx = lax.top_k(g_logits, 1)
    p_group = jnp.take_along_axis(g_prob, g_idx, axis=-1)
    e_all = jnp.einsum('nd,gde->nge', tf, re_w.astype(jnp.float32)) + re_b.astype(jnp.float32)
    e_logits = jnp.take_along_axis(e_all, g_idx[:, :, None], axis=1)[:, 0]
    top_v, top_i = lax.top_k(e_logits, TOP_K_INNER)
    w = p_group * jax.nn.softmax(top_v, axis=-1)
    eid = g_idx * EXPERTS_PER_GROUP + top_i
    combine = jnp.einsum('nk,nke->en', w, jax.nn.one_hot(eid, N_EXPERTS, dtype=jnp.float32))

    def expert_step(acc, xs):
        wg, wu, wd, ce = xs
        hid = jax.nn.silu(t @ wg) * (t @ wu)
        return acc + ce[:, None] * (hid @ wd).astype(jnp.float32), None

    acc, _ = lax.scan(expert_step, jnp.zeros((B * T, D), jnp.float32), (w_gate, w_up, w_down, combine))
    return acc.astype(h.dtype).reshape(B, T, D)


def setup_inputs(seed: int = 0) -> dict:
    key = jax.random.key(seed)
    ks = list(jax.random.split(key, 32))

    def nrm(shape, s):
        return jax.random.normal(ks.pop(), shape, jnp.float32) * s

    D = D_MODEL
    cache_rows = min(BAND_PAST, PAST_LEN)
    return {
        'x_prompt': nrm((BATCH, SEQ, D), 1.0),
        'x_sample': nrm((DEC_BATCH, DEC_SEQ, D), 1.0),
        'c_prompt': nrm((BATCH, D), 1.0),
        'c_sample': nrm((DEC_BATCH, D), 1.0),
        'state_pool': nrm((N_POOL_LAYERS, DEC_BATCH, POOL_STATE, D), 1.0),
        'cache_k': nrm((N_ATTN_LAYERS, DEC_BATCH, cache_rows, N_HEADS, HEAD_DIM), 1.0),
        'cache_v': nrm((N_ATTN_LAYERS, DEC_BATCH, cache_rows, N_HEADS, HEAD_DIM), 1.0),
        'ada_w': nrm((DEPTH, D, 6 * D), 0.1 * D ** -0.5),
        'ada_b': nrm((DEPTH, 6 * D), 0.01),
        'ln_g': 1.0 + nrm((DEPTH, 2, D), 0.02),
        'ln_b': nrm((DEPTH, 2, D), 0.02),
        'pool_w': nrm((N_POOL_LAYERS, N_POOL_GROUPS, POOL_GROUP_DIM, POOL_GROUP_DIM), POOL_GROUP_DIM ** -0.5 * DEEPNORM_BETA),
        'pool_scale': 1.0 + nrm((N_POOL_LAYERS, D), 0.1),
        'attn_w_qkv': nrm((N_ATTN_LAYERS, D, 3 * D), D ** -0.5),
        'attn_w_o': nrm((N_ATTN_LAYERS, D, D), D ** -0.5 * DEEPNORM_BETA),
        'attn_rel_bias': nrm((N_ATTN_LAYERS, N_HEADS, 2 * REL_CLIP + 1), 0.2),
        'router_group_w': nrm((DEPTH, D, N_GROUPS), D ** -0.5),
        'router_group_b': nrm((DEPTH, N_GROUPS), 0.01),
        'router_expert_w': nrm((DEPTH, N_GROUPS, D, EXPERTS_PER_GROUP), D ** -0.5),
        'router_expert_b': nrm((DEPTH, N_GROUPS, EXPERTS_PER_GROUP), 0.01),
        'moe_w_gate': nrm((DEPTH, N_EXPERTS, D, D_EXPERT), D ** -0.5),
        'moe_w_up': nrm((DEPTH, N_EXPERTS, D, D_EXPERT), D ** -0.5),
        'moe_w_down': nrm((DEPTH, N_EXPERTS, D_EXPERT, D), D_EXPERT ** -0.5 * DEEPNORM_BETA),
    }


def reference(x_prompt, x_sample, c_prompt, c_sample, state_pool, cache_k, cache_v, ada_w, ada_b, ln_g, ln_b,
              pool_w, pool_scale, attn_w_qkv, attn_w_o, attn_rel_bias, router_group_w, router_group_b,
              router_expert_w, router_expert_b, moe_w_gate, moe_w_up, moe_w_down):
    xp, xs = x_prompt, x_sample
    pool_p, pool_s, kp_l, vp_l, ks_l, vs_l = [], [], [], [], [], []
    for l in range(DEPTH):
        sh1p, sc1p, g1p, sh2p, sc2p, g2p = adaln_params(c_prompt, ada_w[l], ada_b[l])
        sh1s, sc1s, g1s, sh2s, sc2s, g2s = adaln_params(c_sample, ada_w[l], ada_b[l])
        hp = modulate(xp, sh1p, sc1p)
        hs = modulate(xs, sh1s, sc1s)
        j = l // N_MIXERS
        if l % N_MIXERS == 0:
            zero_hist = jnp.zeros((xp.shape[0], POOL_STATE, D_MODEL), xp.dtype)
            mp, stp = pool_mixer(hp, zero_hist, 0, pool_w[j], pool_scale[j])
            ms, sts = pool_mixer(hs, state_pool[j], PAST_LEN, pool_w[j], pool_scale[j])
            pool_p.append(stp)
            pool_s.append(sts)
        else:
            mp, kp, vp = attn_prompt(hp, attn_w_qkv[j], attn_w_o[j], attn_rel_bias[j])
            ms, kn, vn = attn_sample(hs, cache_k[j], cache_v[j], attn_w_qkv[j], attn_w_o[j], attn_rel_bias[j])
            kp_l.append(kp)
            vp_l.append(vp)
            ks_l.append(kn)
            vs_l.append(vn)
        xp = post_norm(xp, mp, g1p, ln_g[l, 0], ln_b[l, 0])
        xs = post_norm(xs, ms, g1s, ln_g[l, 0], ln_b[l, 0])
        fp = hier_moe(modulate(xp, sh2p, sc2p), router_group_w[l], router_group_b[l], router_expert_w[l],
                      router_expert_b[l], moe_w_gate[l], moe_w_up[l], moe_w_down[l])
        fs = hier_moe(modulate(xs, sh2s, sc2s), router_group_w[l], router_group_b[l], router_expert_w[l],
                      router_expert_b[l], moe_w_gate[l], moe_w_up[l], moe_w_down[l])
        xp = post_norm(xp, fp, g2p, ln_g[l, 1], ln_b[l, 1])
        xs = post_norm(xs, fs, g2s, ln_g[l, 1], ln_b[l, 1])
    return (xp, xs, jnp.stack(pool_p), jnp.stack(kp_l), jnp.stack(vp_l), jnp.stack(pool_s), jnp.stack(ks_l), jnp.stack(vs_l))
```

```python
import functools

import jax
import jax.numpy as jnp
from jax import lax
from jax.experimental import pallas as pl
from jax.experimental.pallas import tpu as pltpu

D_MODEL = 4096
BATCH = 2
SEQ = 8192
DEPTH = 2
DEC_BATCH = 16
DEC_SEQ = 16
PAST_LEN = 4096
CHUNK = 64
POOL_WINDOWS = (2, 4, 8, 16)
POOL_STATE = 15
HEAD_DIM = 64
N_LEFT_CHUNKS = 8
REL_CLIP = 256
N_GROUPS = 4
EXPERTS_PER_GROUP = 8
D_EXPERT = 512
LN_EPS = 1e-5
NEG_INF = -1e30

ROW_TILE = 256
HALO = 16
EXPERT_TILE = 256
MM_TN = 1024
ATTN_Q = 256
ATTN_LANES = 512
ROUTER_LANES = 128
EXPERT_LANE0 = 32
VMEM_LIMIT = 56 * 1024 * 1024

F32 = jnp.float32
BF16 = jnp.bfloat16


def _n_experts():
    return N_GROUPS * EXPERTS_PER_GROUP


def _n_prompt():
    return BATCH * SEQ


def _n_sample():
    return DEC_BATCH * DEC_SEQ


def _n_tot():
    return _n_prompt() + _n_sample()


def _max_expert_tiles():
    return (2 * _n_tot()) // EXPERT_TILE + _n_experts()


def _deepnorm_alpha():
    return (2 * DEPTH) ** 0.25


def _params(vmem=VMEM_LIMIT):
    return pltpu.CompilerParams(vmem_limit_bytes=vmem)


def _layer_norm(z, g, b):
    mu = jnp.mean(z, axis=-1, keepdims=True)
    zc = z - mu
    var = jnp.mean(zc * zc, axis=-1, keepdims=True)
    return zc * lax.rsqrt(var + LN_EPS) * g + b


def _ada_kernel(c_ref, w_ref, b_ref, o_ref):
    c = c_ref[...]
    s = (c / (1.0 + jnp.exp(-c))).astype(BF16)
    w = w_ref[0].astype(BF16)
    o_ref[0] = jnp.dot(s, w, preferred_element_type=F32) + b_ref[0]


def _ada_params(c_all, ada_w, ada_b):
    r, d = c_all.shape
    n = ada_w.shape[-1]
    tn = min(512, n)
    return pl.pallas_call(
        _ada_kernel,
        grid=(DEPTH, n // tn),
        in_specs=[
            pl.BlockSpec((r, d), lambda l, j: (0, 0)),
            pl.BlockSpec((1, d, tn), lambda l, j: (l, 0, j)),
            pl.BlockSpec((1, 1, tn), lambda l, j: (l, 0, j)),
        ],
        out_specs=pl.BlockSpec((1, r, tn), lambda l, j: (l, 0, j)),
        out_shape=jax.ShapeDtypeStruct((DEPTH, r, n), F32),
        compiler_params=_params(),
    )(c_all, ada_w, ada_b.reshape(DEPTH, 1, n))


def _pool_kernel(x_ref, halo_ref, p_ref, w_ref, vec_ref, *rest, tm, tiles_per_seq, pos0, zero_first, aliased):
    if aliased:
        rest = rest[2:]
    x1_ref, t1_ref, st_ref, hbuf = rest
    i = pl.program_id(0)
    j = i % tiles_per_seq
    sh1 = p_ref[0, 0:1, :]
    sc1 = p_ref[0, 1:2, :]
    g1 = p_ref[0, 2:3, :]
    sh2 = p_ref[0, 3:4, :]
    sc2 = p_ref[0, 4:5, :]
    ls = vec_ref[0:1, :]
    ln_g = vec_ref[1:2, :]
    ln_b = vec_ref[2:3, :]

    x = x_ref[...]
    h = x * (1.0 + sc1) + sh1
    if zero_first:
        hh = halo_ref[...] * (1.0 + sc1) + sh1
        hh = jnp.where(j == 0, 0.0, hh)
    else:
        hh = halo_ref[...]
    hbuf[0:HALO, :] = hh
    hbuf[HALO:, :] = h

    pos = pos0 + j * tm + lax.broadcasted_iota(jnp.int32, (tm, 1), 0)
    gc = D_MODEL // len(POOL_WINDOWS)
    ys = []
    for g, w in enumerate(POOL_WINDOWS):
        s = hbuf[:, g * gc:(g + 1) * gc]
        k = 1
        while k < w:
            s = s + pltpu.roll(s, k, 0)
            k *= 2
        inv = 1.0 / jnp.minimum(pos + 1, w).astype(F32)
        pooled = s[HALO:, :] * inv - hbuf[HALO:, g * gc:(g + 1) * gc]
        ys.append(jnp.dot(pooled.astype(BF16), w_ref[g], preferred_element_type=F32))
    m = jnp.concatenate(ys, axis=-1) * ls
    x1 = _layer_norm(_deepnorm_alpha() * x + (1.0 + g1) * m, ln_g, ln_b)
    x1_ref[...] = x1
    t1_ref[...] = x1 * (1.0 + sc2) + sh2

    @pl.when(j == tiles_per_seq - 1)
    def _():
        st_ref[0] = hbuf[tm:, :]


def _pool_layer(xp, xs, hist_pad, pp, ps, pool_w, vecs):
    d = D_MODEL
    n_tot = _n_tot()
    gc = d // len(POOL_WINDOWS)
    np_tiles = _n_prompt() // ROW_TILE
    tps = SEQ // ROW_TILE
    hb = ROW_TILE // HALO
    common = [
        pl.BlockSpec((len(POOL_WINDOWS), gc, gc), lambda i: (0, 0, 0)),
        pl.BlockSpec((3, d), lambda i: (0, 0)),
    ]
    x1, t1, st_p = pl.pallas_call(
        functools.partial(_pool_kernel, tm=ROW_TILE, tiles_per_seq=tps, pos0=0, zero_first=True, aliased=False),
        grid=(np_tiles,),
        in_specs=[
            pl.BlockSpec((ROW_TILE, d), lambda i: (i, 0)),
            pl.BlockSpec((HALO, d), lambda i: (jnp.maximum(i * hb - 1, 0), 0)),
            pl.BlockSpec((1, 5, d), lambda i: (i // tps, 0, 0)),
        ] + common,
        out_specs=[
            pl.BlockSpec((ROW_TILE, d), lambda i: (i, 0)),
            pl.BlockSpec((ROW_TILE, d), lambda i: (i, 0)),
            pl.BlockSpec((1, HALO, d), lambda i: (i // tps, 0, 0)),
        ],
        out_shape=[
            jax.ShapeDtypeStruct((n_tot, d), F32),
            jax.ShapeDtypeStruct((n_tot, d), F32),
            jax.ShapeDtypeStruct((BATCH, HALO, d), F32),
        ],
        scratch_shapes=[pltpu.VMEM((ROW_TILE + HALO, d), F32)],
        compiler_params=_params(),
    )(xp, xp, pp, pool_w, vecs)

    off = _n_prompt() // DEC_SEQ
    x1, t1, h_s = pl.pallas_call(
        functools.partial(_pool_kernel, tm=DEC_SEQ, tiles_per_seq=1, pos0=PAST_LEN, zero_first=False, aliased=True),
        grid=(DEC_BATCH,),
        in_specs=[
            pl.BlockSpec((DEC_SEQ, d), lambda i: (i, 0)),
            pl.BlockSpec((HALO, d), lambda i: (i, 0)),
            pl.BlockSpec((1, 5, d), lambda i: (i, 0, 0)),
        ] + common + [
            pl.BlockSpec(memory_space=pl.ANY),
            pl.BlockSpec(memory_space=pl.ANY),
        ],
        out_specs=[
            pl.BlockSpec((DEC_SEQ, d), lambda i: (off + i, 0)),
            pl.BlockSpec((DEC_SEQ, d), lambda i: (off + i, 0)),
            pl.BlockSpec((1, HALO, d), lambda i: (i, 0, 0)),
        ],
        out_shape=[
            jax.ShapeDtypeStruct((n_tot, d), F32),
            jax.ShapeDtypeStruct((n_tot, d), F32),
            jax.ShapeDtypeStruct((DEC_BATCH, HALO, d), F32),
        ],
        scratch_shapes=[pltpu.VMEM((DEC_SEQ + HALO, d), F32)],
        input_output_aliases={5: 0, 6: 1},
        compiler_params=_params(),
    )(xs, hist_pad, ps, pool_w, vecs, x1, t1)
    return x1, t1, st_p, h_s


def _router_kernel(t_ref, wr_ref, br_ref, carry_ref, *rest, tm, aliased):
    if aliased:
        rest = rest[2:]
    ri_ref, rw_ref, cnt_out_ref, cnt = rest
    i = pl.program_id(0)

    @pl.when(i == 0)
    def _():
        cnt[...] = carry_ref[...]

    logits = jnp.dot(t_ref[...], wr_ref[...], preferred_element_type=F32,
                     precision=lax.Precision.HIGHEST) + br_ref[...]
    lane = lax.broadcasted_iota(jnp.int32, (tm, ROUTER_LANES), 1).astype(F32)
    big = float(ROUTER_LANES)
    ninf = -jnp.inf

    lg = jnp.where(lane < N_GROUPS, logits, ninf)
    mg = jnp.max(lg, axis=-1, keepdims=True)
    gidx = jnp.min(jnp.where(lg == mg, lane, big), axis=-1, keepdims=True)
    p_group = 1.0 / jnp.sum(jnp.exp(lg - mg), axis=-1, keepdims=True)

    lo = EXPERT_LANE0 + gidx * EXPERTS_PER_GROUP
    le = jnp.where((lane >= lo) & (lane < lo + EXPERTS_PER_GROUP), logits, ninf)
    m1 = jnp.max(le, axis=-1, keepdims=True)
    i1 = jnp.min(jnp.where(le == m1, lane, big), axis=-1, keepdims=True)
    le2 = jnp.where(lane == i1, ninf, le)
    m2 = jnp.max(le2, axis=-1, keepdims=True)
    i2 = jnp.min(jnp.where(le2 == m2, lane, big), axis=-1, keepdims=True)
    e2 = jnp.exp(m2 - m1)
    w1 = p_group / (1.0 + e2)
    w2 = p_group * e2 / (1.0 + e2)

    hit1 = lane == i1
    hit2 = lane == i2
    onehot = jnp.where(hit1 | hit2, 1.0, 0.0)
    row = lax.broadcasted_iota(jnp.int32, (tm, tm), 0)
    col = lax.broadcasted_iota(jnp.int32, (tm, tm), 1)
    tri = jnp.where(col < row, 1.0, 0.0).astype(BF16)
    before = jnp.dot(tri, onehot.astype(BF16), preferred_element_type=F32) + cnt[...]
    r1 = jnp.sum(jnp.where(hit1, before, 0.0), axis=-1, keepdims=True)
    r2 = jnp.sum(jnp.where(hit2, before, 0.0), axis=-1, keepdims=True)
    cnt[...] = cnt[...] + jnp.sum(onehot, axis=0, keepdims=True)

    zero = jnp.zeros_like(logits)
    ints = jnp.where(lane == 0, i1 - EXPERT_LANE0,
                     jnp.where(lane == 1, i2 - EXPERT_LANE0,
                               jnp.where(lane == 2, r1, jnp.where(lane == 3, r2, zero))))
    ri_ref[...] = ints.astype(jnp.int32)
    rw_ref[...] = jnp.where(lane == 0, w1, jnp.where(lane == 1, w2, zero))
    cnt_out_ref[...] = cnt[...]


def _router(t, wr, br):
    d = D_MODEL
    n_tot = _n_tot()
    outs_shape = [
        jax.ShapeDtypeStruct((n_tot, ROUTER_LANES), jnp.int32),
        jax.ShapeDtypeStruct((n_tot, ROUTER_LANES), F32),
        jax.ShapeDtypeStruct((1, ROUTER_LANES), F32),
    ]
    fixed = [
        pl.BlockSpec((d, ROUTER_LANES), lambda i: (0, 0)),
        pl.BlockSpec((1, ROUTER_LANES), lambda i: (0, 0)),
        pl.BlockSpec((1, ROUTER_LANES), lambda i: (0, 0)),
    ]
    ri, rw, cnt = pl.pallas_call(
        functools.partial(_router_kernel, tm=ROW_TILE, aliased=False),
        grid=(_n_prompt() // ROW_TILE,),
        in_specs=[pl.BlockSpec((ROW_TILE, d), lambda i: (i, 0))] + fixed,
        out_specs=[
            pl.BlockSpec((ROW_TILE, ROUTER_LANES), lambda i: (i, 0)),
            pl.BlockSpec((ROW_TILE, ROUTER_LANES), lambda i: (i, 0)),
            pl.BlockSpec((1, ROUTER_LANES), lambda i: (0, 0)),
        ],
        out_shape=outs_shape,
        scratch_shapes=[pltpu.VMEM((1, ROUTER_LANES), F32)],
        compiler_params=_params(),
    )(t, wr, br, jnp.zeros((1, ROUTER_LANES), F32))
    off = _n_prompt() // DEC_SEQ
    ri, rw, cnt = pl.pallas_call(
        functools.partial(_router_kernel, tm=DEC_SEQ, aliased=True),
        grid=(DEC_BATCH,),
        in_specs=[pl.BlockSpec((DEC_SEQ, d), lambda i: (off + i, 0))] + fixed + [
            pl.BlockSpec(memory_space=pl.ANY),
            pl.BlockSpec(memory_space=pl.ANY),
        ],
        out_specs=[
            pl.BlockSpec((DEC_SEQ, ROUTER_LANES), lambda i: (off + i, 0)),
            pl.BlockSpec((DEC_SEQ, ROUTER_LANES), lambda i: (off + i, 0)),
            pl.BlockSpec((1, ROUTER_LANES), lambda i: (0, 0)),
        ],
        out_shape=outs_shape,
        scratch_shapes=[pltpu.VMEM((1, ROUTER_LANES), F32)],
        input_output_aliases={4: 0, 5: 1},
        compiler_params=_params(),
    )(t, wr, br, cnt, ri, rw)
    return ri, rw, cnt


def _dispatch_kernel(t_ref, pos_ref, xs_in_ref, xs_ref, sem, *, tm):
    del xs_in_ref

    def copy(r, k):
        dst = pos_ref[0, 0, 2 * r + k]
        return pltpu.make_async_copy(t_ref.at[pl.ds(r, 1)], xs_ref.at[pl.ds(dst, 1)], sem)

    def start(r, c):
        copy(r, 0).start()
        copy(r, 1).start()
        return c

    def wait(r, c):
        copy(r, 0).wait()
        copy(r, 1).wait()
        return c

    lax.fori_loop(0, tm, start, 0)
    lax.fori_loop(0, tm, wait, 0)


def _dispatch(t, pos):
    d = D_MODEL
    p_max = _max_expert_tiles() * EXPERT_TILE
    xs = jnp.zeros((p_max, d), F32)
    np_tiles = _n_prompt() // ROW_TILE
    pos_p = pos[:_n_prompt()].reshape(np_tiles, 1, 2 * ROW_TILE)
    pos_s = pos[_n_prompt():].reshape(DEC_BATCH, 1, 2 * DEC_SEQ)
    off = _n_prompt() // DEC_SEQ
    for tm, grid, pp, roff in ((ROW_TILE, np_tiles, pos_p, 0), (DEC_SEQ, DEC_BATCH, pos_s, off)):
        xs = pl.pallas_call(
            functools.partial(_dispatch_kernel, tm=tm),
            grid=(grid,),
            in_specs=[
                pl.BlockSpec((tm, d), lambda i, roff=roff: (roff + i, 0)),
                pl.BlockSpec((1, 1, 2 * tm), lambda i: (i, 0, 0), memory_space=pltpu.SMEM),
                pl.BlockSpec(memory_space=pl.ANY),
            ],
            out_specs=pl.BlockSpec(memory_space=pl.ANY),
            out_shape=jax.ShapeDtypeStruct((p_max, d), F32),
            scratch_shapes=[pltpu.SemaphoreType.DMA(())],
            input_output_aliases={2: 0},
            compiler_params=_params(),
        )(t, pp, xs)
    return xs


def _gmm_kernel(te_ref, nu_ref, xs_ref, wg_ref, wu_ref, wd_ref, ys_ref):
    del te_ref

    @pl.when(pl.program_id(0) < nu_ref[0])
    def _():
        x = xs_ref[...].astype(BF16)
        a = jnp.dot(x, wg_ref[0], preferred_element_type=F32)
        b = jnp.dot(x, wu_ref[0], preferred_element_type=F32)
        hid = (a / (1.0 + jnp.exp(-a))) * b
        ys_ref[...] = jnp.dot(hid.astype(BF16), wd_ref[0], preferred_element_type=F32)


def _gmm(xs, tile_expert, n_used, wg, wu, wd):
    d = D_MODEL
    tiles = _max_expert_tiles()

    def row(i, te, nu):
        return (jnp.minimum(i, nu[0] - 1), 0)

    def wsel(i, te, nu):
        return (te[jnp.minimum(i, nu[0] - 1)], 0, 0)

    return pl.pallas_call(
        _gmm_kernel,
        grid_spec=pltpu.PrefetchScalarGridSpec(
            num_scalar_prefetch=2,
            grid=(tiles,),
            in_specs=[
                pl.BlockSpec((EXPERT_TILE, d), row),
                pl.BlockSpec((1, d, D_EXPERT), wsel),
                pl.BlockSpec((1, d, D_EXPERT), wsel),
                pl.BlockSpec((1, D_EXPERT, d), wsel),
            ],
            out_specs=pl.BlockSpec((EXPERT_TILE, d), row),
        ),
        out_shape=jax.ShapeDtypeStruct((tiles * EXPERT_TILE, d), F32),
        compiler_params=_params(),
    )(tile_expert, n_used, xs, wg, wu, wd)


def _post_kernel(x_ref, pos_ref, rw_ref, p_ref, vec_ref, ys_ref, *rest, tm, emit_next, aliased):
    if aliased:
        rest = rest[(2 if emit_next else 1):]
    if emit_next:
        xo_ref, an_ref, ybuf, sem = rest
    else:
        xo_ref, ybuf, sem = rest

    def copy(r, k):
        src = pos_ref[0, 0, 2 * r + k]
        return pltpu.make_async_copy(ys_ref.at[pl.ds(src, 1)], ybuf.at[k, pl.ds(r, 1)], sem)

    def start(r, c):
        copy(r, 0).start()
        copy(r, 1).start()
        return c

    def wait(r, c):
        copy(r, 0).wait()
        copy(r, 1).wait()
        return c

    lax.fori_loop(0, tm, start, 0)
    lax.fori_loop(0, tm, wait, 0)

    g2 = p_ref[0, 0:1, :]
    w0 = rw_ref[:, 0:1]
    w1 = rw_ref[:, 1:2]
    f = w0 * ybuf[0] + w1 * ybuf[1]
    xo = _layer_norm(_deepnorm_alpha() * x_ref[...] + (1.0 + g2) * f, vec_ref[0:1, :], vec_ref[1:2, :])
    xo_ref[...] = xo
    if emit_next:
        an_ref[...] = (xo * (1.0 + p_ref[0, 2:3, :]) + p_ref[0, 1:2, :]).astype(BF16)


def _post(x, pos, rw, ys, pp, ps, vecs, emit_next):
    d = D_MODEL
    n_p, n_s, n_tot = _n_prompt(), _n_sample(), _n_tot()
    np_tiles = n_p // ROW_TILE
    tps = SEQ // ROW_TILE
    pos_p = pos[:n_p].reshape(np_tiles, 1, 2 * ROW_TILE)
    pos_s = pos[n_p:].reshape(DEC_BATCH, 1, 2 * DEC_SEQ)
    off = n_p // DEC_SEQ
    npar = pp.shape[1]

    def call(tm, grid, roff, posb, par, bmap, prev):
        aliased = prev is not None
        in_specs = [
            pl.BlockSpec((tm, d), lambda i: (roff + i, 0)),
            pl.BlockSpec((1, 1, 2 * tm), lambda i: (i, 0, 0), memory_space=pltpu.SMEM),
            pl.BlockSpec((tm, ROUTER_LANES), lambda i: (roff + i, 0)),
            pl.BlockSpec((1, npar, d), lambda i: (bmap(i), 0, 0)),
            pl.BlockSpec((2, d), lambda i: (0, 0)),
            pl.BlockSpec(memory_space=pl.ANY),
        ]
        args = [x, posb, rw, par, vecs, ys]
        if emit_next:
            oroff, rows = roff, n_tot
        else:
            oroff, rows = 0, grid * tm
        out_specs = [pl.BlockSpec((tm, d), lambda i: (oroff + i, 0))]
        out_shape = [jax.ShapeDtypeStruct((rows, d), F32)]
        if emit_next:
            out_specs.append(pl.BlockSpec((tm, d), lambda i: (oroff + i, 0)))
            out_shape.append(jax.ShapeDtypeStruct((rows, d), BF16))
        aliases = {}
        if aliased:
            for k, a in enumerate(prev):
                in_specs.append(pl.BlockSpec(memory_space=pl.ANY))
                args.append(a)
                aliases[6 + k] = k
        return pl.pallas_call(
            functools.partial(_post_kernel, tm=tm, emit_next=emit_next, aliased=aliased),
            grid=(grid,),
            in_specs=in_specs,
            out_specs=out_specs,
            out_shape=out_shape,
            scratch_shapes=[pltpu.VMEM((2, tm, d), F32), pltpu.SemaphoreType.DMA(())],
            input_output_aliases=aliases,
            compiler_params=_params(),
        )(*args)

    outs_p = call(ROW_TILE, np_tiles, 0, pos_p, pp, lambda i: i // tps, None)
    if emit_next:
        return call(DEC_SEQ, DEC_BATCH, off, pos_s, ps, lambda i: i, outs_p)
    outs_s = call(DEC_SEQ, DEC_BATCH, off, pos_s, ps, lambda i: i, None)
    return outs_p[0], outs_s[0]


def _moe(t, x, layer_w, pp, ps, vecs, emit_next):
    wr, br, wg, wu, wd = layer_w
    ri, rw, cnt = _router(t, wr, br)
    ne = _n_experts()
    counts = cnt[0, EXPERT_LANE0:EXPERT_LANE0 + ne].astype(jnp.int32)
    tiles_e = (counts + EXPERT_TILE - 1) // EXPERT_TILE
    tile_end = jnp.cumsum(tiles_e)
    pad_off = (tile_end - tiles_e) * EXPERT_TILE
    n_used = tile_end[-1:].astype(jnp.int32)
    tile_ids = jnp.arange(_max_expert_tiles(), dtype=jnp.int32)
    tile_expert = jnp.minimum(jnp.sum(tile_ids[:, None] >= tile_end[None, :], axis=1), ne - 1).astype(jnp.int32)
    eid = ri[:, 0:2]
    rank = ri[:, 2:4]
    onehot = eid[:, :, None] == jnp.arange(ne, dtype=jnp.int32)[None, None, :]
    pos = jnp.sum(jnp.where(onehot, pad_off[None, None, :], 0), axis=-1) + rank
    xs = _dispatch(t, pos.astype(jnp.int32))
    ys = _gmm(xs, tile_expert, n_used, wg, wu, wd)
    return _post(x, pos.astype(jnp.int32), rw, ys, pp, ps, vecs, emit_next)


def _mm_kernel(a_ref, w_ref, *out_refs, keep_tiles):
    acc = jnp.dot(a_ref[...], w_ref[...], preferred_element_type=F32)
    out_refs[0][...] = acc.astype(out_refs[0].dtype)
    if keep_tiles:
        i = pl.program_id(1)
        keep = functools.reduce(jnp.logical_or, [i == k for k in keep_tiles])

        @pl.when(keep)
        def _():
            out_refs[1][...] = acc


def _matmul(a, w, col_off, n_out, out_dtype, keep_tiles=()):
    m, k = a.shape
    tm = ROW_TILE
    tn = min(MM_TN, n_out)
    joff = col_off // tn
    out_specs = [pl.BlockSpec((tm, tn), lambda j, i: (i, j))]
    out_shape = [jax.ShapeDtypeStruct((m, n_out), out_dtype)]
    if keep_tiles:
        def slot(j, i):
            return (sum((i > b).astype(jnp.int32) for b in keep_tiles), j)

        out_specs.append(pl.BlockSpec((tm, tn), slot))
        out_shape.append(jax.ShapeDtypeStruct((len(keep_tiles) * tm, n_out), F32))
    outs = pl.pallas_call(
        functools.partial(_mm_kernel, keep_tiles=tuple(keep_tiles)),
        grid=(n_out // tn, m // tm),
        in_specs=[
            pl.BlockSpec((tm, k), lambda j, i: (i, 0)),
            pl.BlockSpec((k, tn), lambda j, i: (0, joff + j)),
        ],
        out_specs=out_specs,
        out_shape=out_shape,
        compiler_params=_params(),
    )(a, w)
    return outs


def _softmax_pv(s_blocks, pens, v_blocks):
    m = None
    for s, pen in zip(s_blocks, pens):
        rm = jnp.max(s, axis=-1, keepdims=True) + pen
        m = rm if m is None else jnp.maximum(m, rm)
    l = None
    o = None
    for s, pen, v in zip(s_blocks, pens, v_blocks):
        p = jnp.exp(s - (m - pen))
        ls = jnp.sum(p, axis=-1, keepdims=True)
        po = jnp.dot(p.astype(BF16), v, preferred_element_type=F32)
        l = ls if l is None else l + ls
        o = po if o is None else o + po
    return o / l


def _attn_prompt_kernel(q_ref, k0_ref, k1_ref, k2_ref, v0_ref, v1_ref, v2_ref, bias_ref, o_ref):
    cg = pl.program_id(2)
    k_refs = (k0_ref, k1_ref, k2_ref)
    v_refs = (v0_ref, v1_ref, v2_ref)
    pens = (jnp.where(cg >= 2, 0.0, NEG_INF), jnp.where(cg >= 1, 0.0, NEG_INF), 0.0)
    lane = lax.broadcasted_iota(jnp.int32, (ATTN_Q, 2 * HEAD_DIM), 1)
    first = lane < HEAD_DIM
    scale = HEAD_DIM ** -0.5
    for p in range(ATTN_LANES // (2 * HEAD_DIM)):
        cols = slice(p * 2 * HEAD_DIM, (p + 1) * 2 * HEAD_DIM)
        q2 = q_ref[:, cols] * scale
        halves = []
        for hh in range(2):
            qm = jnp.where(first if hh == 0 else ~first, q2, jnp.zeros_like(q2))
            s_blocks = []
            for kb in range(3):
                s = lax.dot_general(qm, k_refs[kb][:, cols], (((1,), (1,)), ((), ())),
                                    preferred_element_type=F32)
                s_blocks.append(s + bias_ref[2 * p + hh, :, kb * ATTN_Q:(kb + 1) * ATTN_Q])
            halves.append(_softmax_pv(s_blocks, pens, [v_refs[kb][:, cols] for kb in range(3)]))
        o_ref[:, cols] = jnp.where(first, halves[0], halves[1]).astype(o_ref.dtype)


def _attn_prompt(q, k, v, bias):
    d = D_MODEL
    cgs = SEQ // ATTN_Q
    hpg = ATTN_LANES // HEAD_DIM

    def kmap(back):
        return lambda hg, b, c: (b * cgs + jnp.maximum(c - back, 0), hg)

    blk = (ATTN_Q, ATTN_LANES)
    return pl.pallas_call(
        _attn_prompt_kernel,
        grid=(d // ATTN_LANES, BATCH, cgs),
        in_specs=[
            pl.BlockSpec(blk, lambda hg, b, c: (b * cgs + c, hg)),
            pl.BlockSpec(blk, kmap(2)), pl.BlockSpec(blk, kmap(1)), pl.BlockSpec(blk, kmap(0)),
            pl.BlockSpec(blk, kmap(2)), pl.BlockSpec(blk, kmap(1)), pl.BlockSpec(blk, kmap(0)),
            pl.BlockSpec((hpg, ATTN_Q, 3 * ATTN_Q), lambda hg, b, c: (hg, 0, 0)),
        ],
        out_specs=pl.BlockSpec(blk, lambda hg, b, c: (b * cgs + c, hg)),
        out_shape=jax.ShapeDtypeStruct((_n_tot(), d), BF16),
        compiler_params=_params(),
    )(q, k, k, k, v, v, v, bias)


def _attn_sample_kernel(q_ref, kn_ref, vn_ref, ck_ref, cv_ref, bc_ref, bn_ref, o_in_ref, o_ref):
    del o_in_ref
    lane = lax.broadcasted_iota(jnp.int32, (DEC_SEQ, 2 * HEAD_DIM), 1)
    first = lane < HEAD_DIM
    scale = HEAD_DIM ** -0.5
    for p in range(ATTN_LANES // (2 * HEAD_DIM)):
        cols = slice(p * 2 * HEAD_DIM, (p + 1) * 2 * HEAD_DIM)
        q2 = q_ref[:, cols] * scale
        kc = ck_ref[0, :, cols].astype(BF16)
        vc = cv_ref[0, :, cols].astype(BF16)
        kn = kn_ref[:, cols]
        vn = vn_ref[:, cols]
        halves = []
        for hh in range(2):
            qm = jnp.where(first if hh == 0 else ~first, q2, jnp.zeros_like(q2))
            dn = (((1,), (1,)), ((), ()))
            s_c = lax.dot_general(qm, kc, dn, preferred_element_type=F32) + bc_ref[2 * p + hh]
            s_n = lax.dot_general(qm, kn, dn, preferred_element_type=F32) + bn_ref[2 * p + hh]
            halves.append(_softmax_pv([s_c, s_n], (0.0, 0.0), [vc, vn]))
        o_ref[:, cols] = jnp.where(first, halves[0], halves[1]).astype(o_ref.dtype)


def _attn_sample(q, k, v, ck, cv, bias_c, bias_n, o):
    d = D_MODEL
    off = _n_prompt() // DEC_SEQ
    hpg = ATTN_LANES // HEAD_DIM
    rows = ck.shape[1]
    new = pl.BlockSpec((DEC_SEQ, ATTN_LANES), lambda hg, b: (off + b, hg))
    cache = pl.BlockSpec((1, rows, ATTN_LANES), lambda hg, b: (b, 0, hg))
    return pl.pallas_call(
        _attn_sample_kernel,
        grid=(d // ATTN_LANES, DEC_BATCH),
        in_specs=[
            new, new, new, cache, cache,
            pl.BlockSpec((hpg, DEC_SEQ, rows), lambda hg, b: (hg, 0, 0)),
            pl.BlockSpec((hpg, DEC_SEQ, DEC_SEQ), lambda hg, b: (hg, 0, 0)),
            pl.BlockSpec(memory_space=pl.ANY),
        ],
        out_specs=new,
        out_shape=jax.ShapeDtypeStruct((_n_tot(), d), BF16),
        input_output_aliases={7: 0},
        compiler_params=_params(),
    )(q, k, v, ck, cv, bias_c, bias_n, o)


def _rel_index(dist):
    return jnp.clip(dist, -REL_CLIP, REL_CLIP) + REL_CLIP


def _attn_biases(rel_bias, cache_rows):
    band_past = N_LEFT_CHUNKS * CHUNK
    qi = jnp.arange(ATTN_Q)[:, None]
    kj = jnp.arange(3 * ATTN_Q)[None, :]
    dist = qi - (kj - 2 * ATTN_Q)
    qc = qi // CHUNK
    kc = kj // CHUNK - (2 * ATTN_Q) // CHUNK
    vis = (kc <= qc) & (kc >= qc - N_LEFT_CHUNKS)
    del band_past
    bias_p = jnp.where(vis[None], rel_bias[:, _rel_index(dist)], NEG_INF)
    si = jnp.arange(DEC_SEQ)[:, None]
    dist_c = si + cache_rows - jnp.arange(cache_rows)[None, :]
    dist_n = si - jnp.arange(DEC_SEQ)[None, :]
    return bias_p, rel_bias[:, _rel_index(dist_c)], rel_bias[:, _rel_index(dist_n)]


def _ln1_kernel(x_ref, m_ref, p_ref, vec_ref, *rest, aliased):
    if aliased:
        rest = rest[2:]
    x1_ref, t1_ref = rest
    g1 = p_ref[0, 0:1, :]
    x1 = _layer_norm(_deepnorm_alpha() * x_ref[...] + (1.0 + g1) * m_ref[...], vec_ref[0:1, :], vec_ref[1:2, :])
    x1_ref[...] = x1
    t1_ref[...] = x1 * (1.0 + p_ref[0, 2:3, :]) + p_ref[0, 1:2, :]


def _ln1(x, m, pp, ps, vecs):
    d = D_MODEL
    n_tot = _n_tot()
    tps = SEQ // ROW_TILE
    off = _n_prompt() // DEC_SEQ
    out_shape = [jax.ShapeDtypeStruct((n_tot, d), F32), jax.ShapeDtypeStruct((n_tot, d), F32)]

    def call(tm, grid, roff, par, bmap, prev):
        blk = pl.BlockSpec((tm, d), lambda i: (roff + i, 0))
        in_specs = [blk, blk, pl.BlockSpec((1, 3, d), lambda i: (bmap(i), 0, 0)), pl.BlockSpec((2, d), lambda i: (0, 0))]
        args = [x, m, par, vecs]
        aliases = {}
        if prev is not None:
            in_specs += [pl.BlockSpec(memory_space=pl.ANY)] * 2
            args += list(prev)
            aliases = {4: 0, 5: 1}
        return pl.pallas_call(
            functools.partial(_ln1_kernel, aliased=prev is not None),
            grid=(grid,),
            in_specs=in_specs,
            out_specs=[blk, blk],
            out_shape=out_shape,
            input_output_aliases=aliases,
            compiler_params=_params(),
        )(*args)

    outs = call(ROW_TILE, _n_prompt() // ROW_TILE, 0, pp, lambda i: i // tps, None)
    return call(DEC_SEQ, DEC_BATCH, off, ps, lambda i: i, outs)


def _split_mods(mods_l):
    return jnp.split(mods_l, 6, axis=-1)


def _stack(rows, names):
    st = jnp.stack(names, axis=1)
    return st[:BATCH], st[BATCH:BATCH + DEC_BATCH]


def _router_weights(rg_w, rg_b, re_w, re_b):
    d = D_MODEL
    ne = _n_experts()
    wr = jnp.zeros((d, ROUTER_LANES), F32)
    wr = wr.at[:, :N_GROUPS].set(rg_w)
    wr = wr.at[:, EXPERT_LANE0:EXPERT_LANE0 + ne].set(jnp.transpose(re_w, (1, 0, 2)).reshape(d, ne))
    br = jnp.zeros((1, ROUTER_LANES), F32)
    br = br.at[0, :N_GROUPS].set(rg_b)
    br = br.at[0, EXPERT_LANE0:EXPERT_LANE0 + ne].set(re_b.reshape(ne))
    return wr, br


def kernel(x_prompt, x_sample, c_prompt, c_sample, state_pool, cache_k, cache_v, ada_w, ada_b, ln_g, ln_b,
           pool_w, pool_scale, attn_w_qkv, attn_w_o, attn_rel_bias, router_group_w, router_group_b,
           router_expert_w, router_expert_b, moe_w_gate, moe_w_up, moe_w_down):
    d = D_MODEL
    n_p = _n_prompt()
    xp = x_prompt.reshape(n_p, d)
    xs = x_sample.reshape(_n_sample(), d)

    n_c = BATCH + DEC_BATCH
    c_rows = 32
    c_all = jnp.concatenate([c_prompt, c_sample, jnp.zeros((c_rows - n_c, d), F32)], axis=0)
    mods = _ada_params(c_all, ada_w, ada_b)
    sh1_0, sc1_0, g1_0, sh2_0, sc2_0, g2_0 = _split_mods(mods[0])
    sh1_1, sc1_1, g1_1, sh2_1, sc2_1, g2_1 = _split_mods(mods[1])

    def moe_weights(l):
        wr, br = _router_weights(router_group_w[l], router_group_b[l], router_expert_w[l], router_expert_b[l])
        return wr, br, moe_w_gate[l].astype(BF16), moe_w_up[l].astype(BF16), moe_w_down[l].astype(BF16)

    hist_pad = jnp.pad(state_pool[0], ((0, 0), (HALO - POOL_STATE, 0), (0, 0))).reshape(DEC_BATCH * HALO, d)
    pp, ps = _stack(None, [sh1_0, sc1_0, g1_0, sh2_0, sc2_0])
    vecs = jnp.stack([pool_scale[0], ln_g[0, 0], ln_b[0, 0]])
    x1, t1, st_p, h_s = _pool_layer(xp, xs, hist_pad, pp, ps, pool_w[0].astype(BF16), vecs)
    pp, ps = _stack(None, [g2_0, sh1_1, sc1_1])
    x2, a2 = _moe(t1, x1, moe_weights(0), pp, ps, jnp.stack([ln_g[0, 1], ln_b[0, 1]]), True)

    w_qkv = attn_w_qkv[0].astype(BF16)
    tiles_per_seq = SEQ // ROW_TILE
    keep_rows = min(N_LEFT_CHUNKS * CHUNK, SEQ)
    kt = keep_rows // ROW_TILE
    keep_tiles = [b * tiles_per_seq + tiles_per_seq - kt + r for b in range(BATCH) for r in range(kt)]
    keep_tiles.append(n_p // ROW_TILE)
    (q,) = _matmul(a2, w_qkv, 0, d, BF16)
    k, k_keep = _matmul(a2, w_qkv, d, d, BF16, keep_tiles)
    v, v_keep = _matmul(a2, w_qkv, 2 * d, d, BF16, keep_tiles)
    cache_rows = cache_k.shape[2]
    bias_p, bias_c, bias_n = _attn_biases(attn_rel_bias[0], cache_rows)
    o = _attn_prompt(q, k, v, bias_p)
    o = _attn_sample(q, k, v, cache_k[0].reshape(DEC_BATCH, cache_rows, d),
                     cache_v[0].reshape(DEC_BATCH, cache_rows, d), bias_c, bias_n, o)
    (m,) = _matmul(o, attn_w_o[0].astype(BF16), 0, d, F32)
    pp, ps = _stack(None, [g1_1, sh2_1, sc2_1])
    x3, t3 = _ln1(x2, m, pp, ps, jnp.stack([ln_g[1, 0], ln_b[1, 0]]))
    pp, ps = _stack(None, [g2_1])
    yp, ys = _moe(t3, x3, moe_weights(1), pp, ps, jnp.stack([ln_g[1, 1], ln_b[1, 1]]), False)

    n_heads = d // HEAD_DIM
    kp_rows = BATCH * keep_rows

    def prompt_cache(keep):
        return keep[:kp_rows].reshape(1, BATCH, keep_rows, n_heads, HEAD_DIM)

    def sample_cache(keep):
        return keep[kp_rows:kp_rows + _n_sample()].reshape(1, DEC_BATCH, DEC_SEQ, n_heads, HEAD_DIM)

    return (
        yp.reshape(BATCH, SEQ, d),
        ys.reshape(DEC_BATCH, DEC_SEQ, d),
        st_p[:, HALO - POOL_STATE:][None],
        prompt_cache(k_keep),
        prompt_cache(v_keep),
        h_s[:, HALO - POOL_STATE:][None],
        sample_cache(k_keep),
        sample_cache(v_keep),
    )
```

```python
import functools
import math

import jax
import jax.numpy as jnp
from jax import lax
from jax.experimental import pallas as pl
from jax.experimental.pallas import tpu as pltpu

D_MODEL = 4096
BATCH = 2
SEQ = 8192
DEPTH = 2
DEC_BATCH = 16
DEC_SEQ = 16
PAST_LEN = 4096
CHUNK = 64
POOL_WINDOWS = (2, 4, 8, 16)
POOL_STATE = 15
HEAD_DIM = 64
N_LEFT_CHUNKS = 8
REL_CLIP = 256
N_GROUPS = 4
EXPERTS_PER_GROUP = 8
D_EXPERT = 512
LN_EPS = 1e-5
NEG_INF = -1e30

ROW_TILE = 256
HALO = 16
EXPERT_TILE = 256
GMM_DOWN_COLS = 512
MM_TN = 1024
ATTN_Q = 256
ATTN_LANES = 512
ROUTER_LANES = 128
EXPERT_LANE0 = 32
VMEM_LIMIT = 56 * 1024 * 1024
LOG2_E = math.log2(math.e)

F32 = jnp.float32
BF16 = jnp.bfloat16


def _n_experts():
    return N_GROUPS * EXPERTS_PER_GROUP


def _n_prompt():
    return BATCH * SEQ


def _n_sample():
    return DEC_BATCH * DEC_SEQ


def _n_tot():
    return _n_prompt() + _n_sample()


def _max_expert_tiles():
    return (2 * _n_tot()) // EXPERT_TILE + _n_experts()


def _deepnorm_alpha():
    return (2 * DEPTH) ** 0.25


def _params(vmem=VMEM_LIMIT):
    return pltpu.CompilerParams(vmem_limit_bytes=vmem)


def _layer_norm(z, g, b):
    mu = jnp.mean(z, axis=-1, keepdims=True)
    zc = z - mu
    var = jnp.mean(zc * zc, axis=-1, keepdims=True)
    return zc * lax.rsqrt(var + LN_EPS) * g + b


def _ada_kernel(c_ref, w_ref, b_ref, o_ref):
    c = c_ref[...]
    s = (c / (1.0 + jnp.exp(-c))).astype(BF16)
    w = w_ref[0].astype(BF16)
    o_ref[0] = jnp.dot(s, w, preferred_element_type=F32) + b_ref[0]


def _ada_params(c_all, ada_w, ada_b):
    r, d = c_all.shape
    n = ada_w.shape[-1]
    tn = min(512, n)
    return pl.pallas_call(
        _ada_kernel,
        grid=(DEPTH, n // tn),
        in_specs=[
            pl.BlockSpec((r, d), lambda l, j: (0, 0)),
            pl.BlockSpec((1, d, tn), lambda l, j: (l, 0, j)),
            pl.BlockSpec((1, 1, tn), lambda l, j: (l, 0, j)),
        ],
        out_specs=pl.BlockSpec((1, r, tn), lambda l, j: (l, 0, j)),
        out_shape=jax.ShapeDtypeStruct((DEPTH, r, n), F32),
        compiler_params=_params(),
        name="ada_params",
    )(c_all, ada_w, ada_b.reshape(DEPTH, 1, n))


def _pool_kernel(x_ref, halo_ref, p_ref, w_ref, vec_ref, *rest, tm, tiles_per_seq, pos0, zero_first, aliased):
    if aliased:
        rest = rest[2:]
    x1_ref, t1_ref, st_ref, hbuf = rest
    i = pl.program_id(0)
    j = i % tiles_per_seq
    sh1 = p_ref[0, 0:1, :]
    sc1 = p_ref[0, 1:2, :]
    g1 = p_ref[0, 2:3, :]
    sh2 = p_ref[0, 3:4, :]
    sc2 = p_ref[0, 4:5, :]
    ls = vec_ref[0:1, :]
    ln_g = vec_ref[1:2, :]
    ln_b = vec_ref[2:3, :]

    x = x_ref[...]
    h = x * (1.0 + sc1) + sh1
    if zero_first:
        hh = halo_ref[...] * (1.0 + sc1) + sh1
        hh = jnp.where(j == 0, 0.0, hh)
    else:
        hh = halo_ref[...]
    hbuf[0:HALO, :] = hh
    hbuf[HALO:, :] = h

    pos = pos0 + j * tm + lax.broadcasted_iota(jnp.int32, (tm, 1), 0)
    gc = D_MODEL // len(POOL_WINDOWS)
    ys = []
    for g, w in enumerate(POOL_WINDOWS):
        s = hbuf[:, g * gc:(g + 1) * gc]
        k = 1
        while k < w:
            s = s + pltpu.roll(s, k, 0)
            k *= 2
        inv = 1.0 / jnp.minimum(pos + 1, w).astype(F32)
        pooled = s[HALO:, :] * inv - hbuf[HALO:, g * gc:(g + 1) * gc]
        ys.append(jnp.dot(pooled.astype(BF16), w_ref[g], preferred_element_type=F32))
    m = jnp.concatenate(ys, axis=-1) * ls
    x1 = _layer_norm(_deepnorm_alpha() * x + (1.0 + g1) * m, ln_g, ln_b)
    x1_ref[...] = x1
    t1_ref[...] = x1 * (1.0 + sc2) + sh2

    @pl.when(j == tiles_per_seq - 1)
    def _():
        st_ref[0] = hbuf[tm:, :]


def _pool_layer(xp, xs, hist_pad, pp, ps, pool_w, vecs):
    d = D_MODEL
    n_tot = _n_tot()
    gc = d // len(POOL_WINDOWS)
    np_tiles = _n_prompt() // ROW_TILE
    tps = SEQ // ROW_TILE
    hb = ROW_TILE // HALO
    common = [
        pl.BlockSpec((len(POOL_WINDOWS), gc, gc), lambda i: (0, 0, 0)),
        pl.BlockSpec((3, d), lambda i: (0, 0)),
    ]
    x1, t1, st_p = pl.pallas_call(
        functools.partial(_pool_kernel, tm=ROW_TILE, tiles_per_seq=tps, pos0=0, zero_first=True, aliased=False),
        grid=(np_tiles,),
        in_specs=[
            pl.BlockSpec((ROW_TILE, d), lambda i: (i, 0)),
            pl.BlockSpec((HALO, d), lambda i: (jnp.maximum(i * hb - 1, 0), 0)),
            pl.BlockSpec((1, 5, d), lambda i: (i // tps, 0, 0)),
        ] + common,
        out_specs=[
            pl.BlockSpec((ROW_TILE, d), lambda i: (i, 0)),
            pl.BlockSpec((ROW_TILE, d), lambda i: (i, 0)),
            pl.BlockSpec((1, HALO, d), lambda i: (i // tps, 0, 0)),
        ],
        out_shape=[
            jax.ShapeDtypeStruct((n_tot, d), F32),
            jax.ShapeDtypeStruct((n_tot, d), F32),
            jax.ShapeDtypeStruct((BATCH, HALO, d), F32),
        ],
        scratch_shapes=[pltpu.VMEM((ROW_TILE + HALO, d), F32)],
        compiler_params=_params(),
        name="pool_prompt",
    )(xp, xp, pp, pool_w, vecs)

    off = _n_prompt() // DEC_SEQ
    x1, t1, h_s = pl.pallas_call(
        functools.partial(_pool_kernel, tm=DEC_SEQ, tiles_per_seq=1, pos0=PAST_LEN, zero_first=False, aliased=True),
        grid=(DEC_BATCH,),
        in_specs=[
            pl.BlockSpec((DEC_SEQ, d), lambda i: (i, 0)),
            pl.BlockSpec((HALO, d), lambda i: (i, 0)),
            pl.BlockSpec((1, 5, d), lambda i: (i, 0, 0)),
        ] + common + [
            pl.BlockSpec(memory_space=pl.ANY),
            pl.BlockSpec(memory_space=pl.ANY),
        ],
        out_specs=[
            pl.BlockSpec((DEC_SEQ, d), lambda i: (off + i, 0)),
            pl.BlockSpec((DEC_SEQ, d), lambda i: (off + i, 0)),
            pl.BlockSpec((1, HALO, d), lambda i: (i, 0, 0)),
        ],
        out_shape=[
            jax.ShapeDtypeStruct((n_tot, d), F32),
            jax.ShapeDtypeStruct((n_tot, d), F32),
            jax.ShapeDtypeStruct((DEC_BATCH, HALO, d), F32),
        ],
        scratch_shapes=[pltpu.VMEM((DEC_SEQ + HALO, d), F32)],
        input_output_aliases={5: 0, 6: 1},
        compiler_params=_params(),
        name="pool_sample",
    )(xs, hist_pad, ps, pool_w, vecs, x1, t1)
    return x1, t1, st_p, h_s


def _router_kernel(t_ref, wr_ref, br_ref, carry_ref, *rest, tm, aliased):
    if aliased:
        rest = rest[2:]
    ri_ref, rw_ref, cnt_out_ref, cnt = rest
    i = pl.program_id(0)

    @pl.when(i == 0)
    def _():
        cnt[...] = carry_ref[...]

    logits = jnp.dot(t_ref[...], wr_ref[...], preferred_element_type=F32,
                     precision=lax.Precision.HIGHEST) + br_ref[...]
    lane = lax.broadcasted_iota(jnp.int32, (tm, ROUTER_LANES), 1).astype(F32)
    big = float(ROUTER_LANES)
    ninf = -jnp.inf

    lg = jnp.where(lane < N_GROUPS, logits, ninf)
    mg = jnp.max(lg, axis=-1, keepdims=True)
    gidx = jnp.min(jnp.where(lg == mg, lane, big), axis=-1, keepdims=True)
    p_group = 1.0 / jnp.sum(jnp.exp(lg - mg), axis=-1, keepdims=True)

    lo = EXPERT_LANE0 + gidx * EXPERTS_PER_GROUP
    le = jnp.where((lane >= lo) & (lane < lo + EXPERTS_PER_GROUP), logits, ninf)
    m1 = jnp.max(le, axis=-1, keepdims=True)
    i1 = jnp.min(jnp.where(le == m1, lane, big), axis=-1, keepdims=True)
    le2 = jnp.where(lane == i1, ninf, le)
    m2 = jnp.max(le2, axis=-1, keepdims=True)
    i2 = jnp.min(jnp.where(le2 == m2, lane, big), axis=-1, keepdims=True)
    e2 = jnp.exp(m2 - m1)
    w1 = p_group / (1.0 + e2)
    w2 = p_group * e2 / (1.0 + e2)

    hit1 = lane == i1
    hit2 = lane == i2
    onehot = jnp.where(hit1 | hit2, 1.0, 0.0)
    row = lax.broadcasted_iota(jnp.int32, (tm, tm), 0)
    col = lax.broadcasted_iota(jnp.int32, (tm, tm), 1)
    tri = jnp.where(col < row, 1.0, 0.0).astype(BF16)
    before = jnp.dot(tri, onehot.astype(BF16), preferred_element_type=F32) + cnt[...]
    r1 = jnp.sum(jnp.where(hit1, before, 0.0), axis=-1, keepdims=True)
    r2 = jnp.sum(jnp.where(hit2, before, 0.0), axis=-1, keepdims=True)
    cnt[...] = cnt[...] + jnp.sum(onehot, axis=0, keepdims=True)

    zero = jnp.zeros_like(logits)
    ints = jnp.where(lane == 0, i1 - EXPERT_LANE0,
                     jnp.where(lane == 1, i2 - EXPERT_LANE0,
                               jnp.where(lane == 2, r1, jnp.where(lane == 3, r2, zero))))
    ri_ref[...] = ints.astype(jnp.int32)
    rw_ref[...] = jnp.where(lane == 0, w1, jnp.where(lane == 1, w2, zero))
    cnt_out_ref[...] = cnt[...]


def _router(t, wr, br):
    d = D_MODEL
    n_tot = _n_tot()
    outs_shape = [
        jax.ShapeDtypeStruct((n_tot, ROUTER_LANES), jnp.int32),
        jax.ShapeDtypeStruct((n_tot, ROUTER_LANES), F32),
        jax.ShapeDtypeStruct((1, ROUTER_LANES), F32),
    ]
    fixed = [
        pl.BlockSpec((d, ROUTER_LANES), lambda i: (0, 0)),
        pl.BlockSpec((1, ROUTER_LANES), lambda i: (0, 0)),
        pl.BlockSpec((1, ROUTER_LANES), lambda i: (0, 0)),
    ]
    ri, rw, cnt = pl.pallas_call(
        functools.partial(_router_kernel, tm=ROW_TILE, aliased=False),
        grid=(_n_prompt() // ROW_TILE,),
        in_specs=[pl.BlockSpec((ROW_TILE, d), lambda i: (i, 0))] + fixed,
        out_specs=[
            pl.BlockSpec((ROW_TILE, ROUTER_LANES), lambda i: (i, 0)),
            pl.BlockSpec((ROW_TILE, ROUTER_LANES), lambda i: (i, 0)),
            pl.BlockSpec((1, ROUTER_LANES), lambda i: (0, 0)),
        ],
        out_shape=outs_shape,
        scratch_shapes=[pltpu.VMEM((1, ROUTER_LANES), F32)],
        compiler_params=_params(),
        name="router_prompt",
    )(t, wr, br, jnp.zeros((1, ROUTER_LANES), F32))
    off = _n_prompt() // DEC_SEQ
    ri, rw, cnt = pl.pallas_call(
        functools.partial(_router_kernel, tm=DEC_SEQ, aliased=True),
        grid=(DEC_BATCH,),
        in_specs=[pl.BlockSpec((DEC_SEQ, d), lambda i: (off + i, 0))] + fixed + [
            pl.BlockSpec(memory_space=pl.ANY),
            pl.BlockSpec(memory_space=pl.ANY),
        ],
        out_specs=[
            pl.BlockSpec((DEC_SEQ, ROUTER_LANES), lambda i: (off + i, 0)),
            pl.BlockSpec((DEC_SEQ, ROUTER_LANES), lambda i: (off + i, 0)),
            pl.BlockSpec((1, ROUTER_LANES), lambda i: (0, 0)),
        ],
        out_shape=outs_shape,
        scratch_shapes=[pltpu.VMEM((1, ROUTER_LANES), F32)],
        input_output_aliases={4: 0, 5: 1},
        compiler_params=_params(),
        name="router_sample",
    )(t, wr, br, cnt, ri, rw)
    return ri, rw, cnt


def _n_slots():
    return (_max_expert_tiles() + 2) * EXPERT_TILE


def _slot_kernel(pos_ref, code_ref):
    n_pairs = 2 * _n_tot()

    def init(s, c):
        code_ref[s] = n_pairs + s
        return c

    def fill(p, c):
        code_ref[pos_ref[p]] = p
        return c

    lax.fori_loop(0, _n_slots(), init, 0, unroll=8)
    lax.fori_loop(0, n_pairs, fill, 0, unroll=8)


def _slot_codes(pos_flat):
    return pl.pallas_call(
        _slot_kernel,
        in_specs=[pl.BlockSpec(memory_space=pltpu.SMEM)],
        out_specs=pl.BlockSpec(memory_space=pltpu.SMEM),
        out_shape=jax.ShapeDtypeStruct((_n_slots(),), jnp.int32),
        name="moe_slots",
    )(pos_flat)


def _gmm_kernel(te_ref, nu_ref, code_ref, t_hbm, wg_ref, wu_ref, wd_ref, y_hbm,
                xbuf0, xbuf1, ybuf0, ybuf1, gsem, ssem):
    del te_ref
    i = pl.program_id(0)
    nu = nu_ref[0]
    tg = EXPERT_TILE
    last_row = _n_tot() - 1
    warm0 = _max_expert_tiles()
    xbufs = (xbuf0, xbuf1)
    ybufs = (ybuf0, ybuf1)

    def gather(tile, k, r):
        src = jnp.minimum(code_ref[tile * tg + r] >> 1, last_row)
        return pltpu.make_async_copy(t_hbm.at[pl.ds(src, 1)], xbufs[k].at[pl.ds(r, 1)], gsem.at[k])

    def scatter(tile, k, r):
        dst = code_ref[tile * tg + r]
        return pltpu.make_async_copy(ybufs[k].at[pl.ds(r, 1)], y_hbm.at[pl.ds(dst, 1)], ssem.at[k])

    def wait_gather(k):
        pltpu.make_async_copy(t_hbm.at[pl.ds(0, tg)], xbufs[k], gsem.at[k]).wait()

    def wait_scatter(k):
        pltpu.make_async_copy(ybufs[k], y_hbm.at[pl.ds(0, tg)], ssem.at[k]).wait()

    @pl.when(i == 0)
    def _():
        ybuf0[...] = jnp.zeros_like(ybuf0)
        ybuf1[...] = jnp.zeros_like(ybuf1)
        for r in range(tg):
            gather(0, 0, r).start()
        for r in range(tg):
            scatter(warm0, 0, r).start()

    def step(k):
        other = 1 - k
        nxt = jnp.minimum(i + 1, nu - 1)
        prev = jnp.where(i == 0, warm0 + 1, i - 1)
        wait_gather(k)
        wait_scatter(k)
        copies = [gather(nxt, other, r) for r in range(tg)] + [scatter(prev, other, r) for r in range(tg)]

        def issue(n):
            for c in copies[:n]:
                c.start()
            del copies[:n]

        pieces = D_MODEL // GMM_DOWN_COLS
        per_piece = len(copies) // (2 + pieces)
        x = xbufs[k][...].astype(BF16)
        a = jnp.dot(x, wg_ref[0, 0], preferred_element_type=F32)
        issue(per_piece)
        b = jnp.dot(x, wu_ref[0, 0], preferred_element_type=F32)
        issue(per_piece)
        hid = ((a / (1.0 + jnp.exp(-a))) * b).astype(BF16)
        for c in range(pieces):
            cols = slice(c * GMM_DOWN_COLS, (c + 1) * GMM_DOWN_COLS)
            ybufs[k][:, cols] = jnp.dot(hid, wd_ref[0, 0, :, cols], preferred_element_type=F32)
            issue(len(copies) if c == pieces - 1 else per_piece)

        @pl.when(i == nu - 1)
        def _():
            for r in range(tg):
                scatter(i, k, r).start()
            wait_scatter(other)
            wait_scatter(k)
            wait_gather(other)

    for k in range(2):
        @pl.when((i < nu) & (i % 2 == k))
        def _(k=k):
            step(k)


def _gmm(t, code, tile_expert, n_used, layer, wg, wu, wd):
    d = D_MODEL
    tiles = _max_expert_tiles()

    def wsel(i, te, nu, code):
        return (layer, te[jnp.minimum(i, nu[0] - 1)], 0, 0)

    return pl.pallas_call(
        _gmm_kernel,
        grid_spec=pltpu.PrefetchScalarGridSpec(
            num_scalar_prefetch=3,
            grid=(tiles,),
            in_specs=[
                pl.BlockSpec(memory_space=pl.ANY),
                pl.BlockSpec((1, 1, d, D_EXPERT), wsel),
                pl.BlockSpec((1, 1, d, D_EXPERT), wsel),
                pl.BlockSpec((1, 1, D_EXPERT, d), wsel),
            ],
            out_specs=pl.BlockSpec(memory_space=pl.ANY),
            scratch_shapes=[
                pltpu.VMEM((EXPERT_TILE, d), F32),
                pltpu.VMEM((EXPERT_TILE, d), F32),
                pltpu.VMEM((EXPERT_TILE, d), F32),
                pltpu.VMEM((EXPERT_TILE, d), F32),
                pltpu.SemaphoreType.DMA((2,)),
                pltpu.SemaphoreType.DMA((2,)),
            ],
        ),
        out_shape=jax.ShapeDtypeStruct((2 * _n_tot() + _n_slots(), d), F32),
        compiler_params=_params(),
        name="moe_experts",
    )(tile_expert, n_used, code, t, wg, wu, wd)


def _post_kernel(x_ref, y_ref, rw_ref, p_ref, vec_ref, *rest, emit_next, aliased):
    if aliased:
        rest = rest[(2 if emit_next else 1):]
    d = D_MODEL
    g2 = p_ref[0, 0:1, :]
    f = rw_ref[:, 0:1] * y_ref[:, :d] + rw_ref[:, 1:2] * y_ref[:, d:]
    xo = _layer_norm(_deepnorm_alpha() * x_ref[...] + (1.0 + g2) * f, vec_ref[0:1, :], vec_ref[1:2, :])
    rest[0][...] = xo
    if emit_next:
        rest[1][...] = (xo * (1.0 + p_ref[0, 2:3, :]) + p_ref[0, 1:2, :]).astype(BF16)


def _post(x, y2, rw, pp, ps, vecs, emit_next):
    d = D_MODEL
    n_p, n_tot = _n_prompt(), _n_tot()
    np_tiles = n_p // ROW_TILE
    tps = SEQ // ROW_TILE
    off = n_p // DEC_SEQ
    npar = pp.shape[1]

    def call(tm, grid, roff, par, bmap, prev, name):
        aliased = prev is not None
        in_specs = [
            pl.BlockSpec((tm, d), lambda i: (roff + i, 0)),
            pl.BlockSpec((tm, 2 * d), lambda i: (roff + i, 0)),
            pl.BlockSpec((tm, ROUTER_LANES), lambda i: (roff + i, 0)),
            pl.BlockSpec((1, npar, d), lambda i: (bmap(i), 0, 0)),
            pl.BlockSpec((2, d), lambda i: (0, 0)),
        ]
        args = [x, y2, rw, par, vecs]
        if emit_next:
            oroff, rows = roff, n_tot
        else:
            oroff, rows = 0, grid * tm
        out_specs = [pl.BlockSpec((tm, d), lambda i: (oroff + i, 0))]
        out_shape = [jax.ShapeDtypeStruct((rows, d), F32)]
        if emit_next:
            out_specs.append(pl.BlockSpec((tm, d), lambda i: (oroff + i, 0)))
            out_shape.append(jax.ShapeDtypeStruct((rows, d), BF16))
        aliases = {}
        if aliased:
            for k, a in enumerate(prev):
                in_specs.append(pl.BlockSpec(memory_space=pl.ANY))
                args.append(a)
                aliases[5 + k] = k
        return pl.pallas_call(
            functools.partial(_post_kernel, emit_next=emit_next, aliased=aliased),
            grid=(grid,),
            in_specs=in_specs,
            out_specs=out_specs,
            out_shape=out_shape,
            input_output_aliases=aliases,
            compiler_params=_params(),
            name=name,
        )(*args)

    outs_p = call(ROW_TILE, np_tiles, 0, pp, lambda i: i // tps, None, "moe_post_prompt")
    if emit_next:
        return call(DEC_SEQ, DEC_BATCH, off, ps, lambda i: i, outs_p, "moe_post_sample")
    outs_s = call(DEC_SEQ, DEC_BATCH, off, ps, lambda i: i, None, "moe_post_sample")
    return outs_p[0], outs_s[0]


def _moe(t, x, layer, router_w, expert_w, pp, ps, vecs, emit_next):
    wr, br = router_w
    ri, rw, cnt = _router(t, wr, br)
    ne = _n_experts()
    counts = cnt[0, EXPERT_LANE0:EXPERT_LANE0 + ne].astype(jnp.int32)
    tiles_e = (counts + EXPERT_TILE - 1) // EXPERT_TILE
    tile_end = jnp.cumsum(tiles_e)
    pad_off = (tile_end - tiles_e) * EXPERT_TILE
    n_used = tile_end[-1:].astype(jnp.int32)
    tile_ids = jnp.arange(_max_expert_tiles(), dtype=jnp.int32)
    tile_expert = jnp.minimum(jnp.sum(tile_ids[:, None] >= tile_end[None, :], axis=1), ne - 1).astype(jnp.int32)
    eid = ri[:, 0:2]
    rank = ri[:, 2:4]
    onehot = eid[:, :, None] == jnp.arange(ne, dtype=jnp.int32)[None, None, :]
    pos = jnp.sum(jnp.where(onehot, pad_off[None, None, :], 0), axis=-1) + rank
    code = _slot_codes(pos.astype(jnp.int32).reshape(-1))
    y = _gmm(t, code, tile_expert, n_used, layer, *expert_w)
    return _post(x, y.reshape(y.shape[0] // 2, 2 * D_MODEL), rw, pp, ps, vecs, emit_next)


def _mm_kernel(a_ref, w_ref, *out_refs, keep_tiles):
    acc = jnp.dot(a_ref[...], w_ref[...], preferred_element_type=F32)
    out_refs[0][...] = acc.astype(out_refs[0].dtype)
    if keep_tiles:
        i = pl.program_id(1)
        keep = functools.reduce(jnp.logical_or, [i == k for k in keep_tiles])

        @pl.when(keep)
        def _():
            out_refs[1][...] = acc


def _matmul(a, w, out_dtype, keep_tiles=(), name="matmul"):
    m, k = a.shape
    n_out = w.shape[1]
    tm = ROW_TILE
    tn = min(MM_TN, n_out)
    out_specs = [pl.BlockSpec((tm, tn), lambda j, i: (i, j))]
    out_shape = [jax.ShapeDtypeStruct((m, n_out), out_dtype)]
    if keep_tiles:
        def slot(j, i):
            return (sum((i > b).astype(jnp.int32) for b in keep_tiles), j)

        out_specs.append(pl.BlockSpec((tm, tn), slot))
        out_shape.append(jax.ShapeDtypeStruct((len(keep_tiles) * tm, n_out), F32))
    outs = pl.pallas_call(
        functools.partial(_mm_kernel, keep_tiles=tuple(keep_tiles)),
        grid=(n_out // tn, m // tm),
        in_specs=[
            pl.BlockSpec((tm, k), lambda j, i: (i, 0)),
            pl.BlockSpec((k, tn), lambda j, i: (0, j)),
        ],
        out_specs=out_specs,
        out_shape=out_shape,
        compiler_params=_params(),
        name=name,
    )(a, w)
    return outs


def _softmax_pv(s_blocks, pens, v_blocks):
    m = None
    for s, pen in zip(s_blocks, pens):
        rm = jnp.max(s, axis=-1, keepdims=True) + pen
        m = rm if m is None else jnp.maximum(m, rm)
    o = None
    for s, pen, v in zip(s_blocks, pens, v_blocks):
        p = jnp.exp2(s - (m - pen))
        po = jnp.dot(p.astype(BF16), v, preferred_element_type=F32)
        o = po if o is None else o + po
    return o


def _merge_heads(first, o_a, o_b):
    o = jnp.where(first, o_a, o_b)
    den = jnp.where(first, pltpu.roll(o_a, HEAD_DIM, 1), pltpu.roll(o_b, HEAD_DIM, 1))
    return o / den


def _attn_prompt_kernel(q_ref, k0_ref, k1_ref, k2_ref, v0_ref, v1_ref, v2_ref, bias_ref, o_ref):
    cg = pl.program_id(2)
    k_refs = (k0_ref, k1_ref, k2_ref)
    v_refs = (v0_ref, v1_ref, v2_ref)
    pens = (jnp.where(cg >= 2, 0.0, NEG_INF), jnp.where(cg >= 1, 0.0, NEG_INF), 0.0)
    lane = lax.broadcasted_iota(jnp.int32, (ATTN_Q, 2 * HEAD_DIM), 1)
    first = lane < HEAD_DIM
    for p in range(ATTN_LANES // (2 * HEAD_DIM)):
        cols = slice(p * 2 * HEAD_DIM, (p + 1) * 2 * HEAD_DIM)
        q2 = q_ref[:, cols]
        v2 = [v_refs[kb][:, cols] for kb in range(3)]
        halves = []
        for hh in range(2):
            mine = first if hh == 0 else ~first
            qm = jnp.where(mine, q2, jnp.zeros_like(q2))
            s_blocks = []
            for kb in range(3):
                s = lax.dot_general(qm, k_refs[kb][:, cols], (((1,), (1,)), ((), ())),
                                    preferred_element_type=F32)
                s_blocks.append(s + bias_ref[2 * p + hh, :, kb * ATTN_Q:(kb + 1) * ATTN_Q])
            vm = [jnp.where(mine, v, jnp.ones_like(v)) for v in v2]
            halves.append(_softmax_pv(s_blocks, pens, vm))
        o_ref[:, cols] = _merge_heads(first, halves[0], halves[1]).astype(o_ref.dtype)


def _attn_prompt(q, k, v, bias):
    d = D_MODEL
    cgs = SEQ // ATTN_Q
    hpg = ATTN_LANES // HEAD_DIM

    def kmap(back):
        return lambda hg, b, c: (b * cgs + jnp.maximum(c - back, 0), hg)

    blk = (ATTN_Q, ATTN_LANES)
    return pl.pallas_call(
        _attn_prompt_kernel,
        grid=(d // ATTN_LANES, BATCH, cgs),
        in_specs=[
            pl.BlockSpec(blk, lambda hg, b, c: (b * cgs + c, hg)),
            pl.BlockSpec(blk, kmap(2)), pl.BlockSpec(blk, kmap(1)), pl.BlockSpec(blk, kmap(0)),
            pl.BlockSpec(blk, kmap(2)), pl.BlockSpec(blk, kmap(1)), pl.BlockSpec(blk, kmap(0)),
            pl.BlockSpec((hpg, ATTN_Q, 3 * ATTN_Q), lambda hg, b, c: (hg, 0, 0)),
        ],
        out_specs=pl.BlockSpec(blk, lambda hg, b, c: (b * cgs + c, hg)),
        out_shape=jax.ShapeDtypeStruct((_n_tot(), d), BF16),
        compiler_params=_params(),
        name="attn_prompt",
    )(q, k, k, k, v, v, v, bias)


def _attn_sample_kernel(q_ref, kn_ref, vn_ref, ck_ref, cv_ref, bc_ref, bn_ref, o_in_ref, o_ref):
    del o_in_ref
    lane = lax.broadcasted_iota(jnp.int32, (DEC_SEQ, 2 * HEAD_DIM), 1)
    first = lane < HEAD_DIM
    for p in range(ATTN_LANES // (2 * HEAD_DIM)):
        cols = slice(p * 2 * HEAD_DIM, (p + 1) * 2 * HEAD_DIM)
        q2 = q_ref[:, cols]
        kc = ck_ref[0, :, cols].astype(BF16)
        vc = cv_ref[0, :, cols].astype(BF16)
        kn = kn_ref[:, cols]
        vn = vn_ref[:, cols]
        halves = []
        for hh in range(2):
            mine = first if hh == 0 else ~first
            qm = jnp.where(mine, q2, jnp.zeros_like(q2))
            dn = (((1,), (1,)), ((), ()))
            s_c = lax.dot_general(qm, kc, dn, preferred_element_type=F32) + bc_ref[2 * p + hh]
            s_n = lax.dot_general(qm, kn, dn, preferred_element_type=F32) + bn_ref[2 * p + hh]
            vm = [jnp.where((lax.broadcasted_iota(jnp.int32, v.shape, 1) < HEAD_DIM) == (hh == 0), v,
                            jnp.ones_like(v)) for v in (vc, vn)]
            halves.append(_softmax_pv([s_c, s_n], (0.0, 0.0), vm))
        o_ref[:, cols] = _merge_heads(first, halves[0], halves[1]).astype(o_ref.dtype)


def _attn_sample(q, k, v, ck, cv, bias_c, bias_n, o):
    d = D_MODEL
    off = _n_prompt() // DEC_SEQ
    hpg = ATTN_LANES // HEAD_DIM
    rows = ck.shape[1]
    new = pl.BlockSpec((DEC_SEQ, ATTN_LANES), lambda hg, b: (off + b, hg))
    cache = pl.BlockSpec((1, rows, ATTN_LANES), lambda hg, b: (b, 0, hg))
    return pl.pallas_call(
        _attn_sample_kernel,
        grid=(d // ATTN_LANES, DEC_BATCH),
        in_specs=[
            new, new, new, cache, cache,
            pl.BlockSpec((hpg, DEC_SEQ, rows), lambda hg, b: (hg, 0, 0)),
            pl.BlockSpec((hpg, DEC_SEQ, DEC_SEQ), lambda hg, b: (hg, 0, 0)),
            pl.BlockSpec(memory_space=pl.ANY),
        ],
        out_specs=new,
        out_shape=jax.ShapeDtypeStruct((_n_tot(), d), BF16),
        input_output_aliases={7: 0},
        compiler_params=_params(),
        name="attn_sample",
    )(q, k, v, ck, cv, bias_c, bias_n, o)


def _rel_index(dist):
    return jnp.clip(dist, -REL_CLIP, REL_CLIP) + REL_CLIP


def _attn_biases(rel_bias, cache_rows):
    tbl = rel_bias * LOG2_E
    n_heads = tbl.shape[0]
    kw = 3 * ATTN_Q
    back = 2 * ATTN_Q
    span = kw + ATTN_Q - 1
    u = tbl[:, _rel_index(back + (ATTN_Q - 1) - jnp.arange(span))]
    rep = jnp.broadcast_to(jnp.pad(u, ((0, 0), (0, 1)))[:, None, :], (n_heads, ATTN_Q, span + 1))
    toep = rep.reshape(n_heads, ATTN_Q * (span + 1))[:, :ATTN_Q * span].reshape(n_heads, ATTN_Q, span)
    qi = jnp.arange(ATTN_Q)[:, None]
    kj = jnp.arange(kw)[None, :]
    qc = qi // CHUNK
    kc = kj // CHUNK - back // CHUNK
    vis = (kc <= qc) & (kc >= qc - N_LEFT_CHUNKS)
    bias_p = jnp.where(vis[None], toep[:, :, ATTN_Q - 1:], NEG_INF)
    si = jnp.arange(DEC_SEQ)[:, None]
    dist_c = si + cache_rows - jnp.arange(cache_rows)[None, :]
    dist_n = si - jnp.arange(DEC_SEQ)[None, :]
    return bias_p, tbl[:, _rel_index(dist_c)], tbl[:, _rel_index(dist_n)]


def _ln1_kernel(x_ref, m_ref, p_ref, vec_ref, *rest, aliased):
    if aliased:
        rest = rest[2:]
    x1_ref, t1_ref = rest
    g1 = p_ref[0, 0:1, :]
    x1 = _layer_norm(_deepnorm_alpha() * x_ref[...] + (1.0 + g1) * m_ref[...], vec_ref[0:1, :], vec_ref[1:2, :])
    x1_ref[...] = x1
    t1_ref[...] = x1 * (1.0 + p_ref[0, 2:3, :]) + p_ref[0, 1:2, :]


def _ln1(x, m, pp, ps, vecs):
    d = D_MODEL
    n_tot = _n_tot()
    tps = SEQ // ROW_TILE
    off = _n_prompt() // DEC_SEQ
    out_shape = [jax.ShapeDtypeStruct((n_tot, d), F32), jax.ShapeDtypeStruct((n_tot, d), F32)]

    def call(tm, grid, roff, par, bmap, prev, name):
        blk = pl.BlockSpec((tm, d), lambda i: (roff + i, 0))
        in_specs = [blk, blk, pl.BlockSpec((1, 3, d), lambda i: (bmap(i), 0, 0)), pl.BlockSpec((2, d), lambda i: (0, 0))]
        args = [x, m, par, vecs]
        aliases = {}
        if prev is not None:
            in_specs += [pl.BlockSpec(memory_space=pl.ANY)] * 2
            args += list(prev)
            aliases = {4: 0, 5: 1}
        return pl.pallas_call(
            functools.partial(_ln1_kernel, aliased=prev is not None),
            grid=(grid,),
            in_specs=in_specs,
            out_specs=[blk, blk],
            out_shape=out_shape,
            input_output_aliases=aliases,
            compiler_params=_params(),
            name=name,
        )(*args)

    outs = call(ROW_TILE, _n_prompt() // ROW_TILE, 0, pp, lambda i: i // tps, None, "attn_post_prompt")
    return call(DEC_SEQ, DEC_BATCH, off, ps, lambda i: i, outs, "attn_post_sample")


def _split_mods(mods_l):
    return jnp.split(mods_l, 6, axis=-1)


def _stack(vectors):
    st = jnp.stack(vectors, axis=1)
    return st[:BATCH], st[BATCH:BATCH + DEC_BATCH]


def _router_weights(rg_w, rg_b, re_w, re_b):
    d = D_MODEL
    ne = _n_experts()
    wr = jnp.zeros((d, ROUTER_LANES), F32)
    wr = wr.at[:, :N_GROUPS].set(rg_w)
    wr = wr.at[:, EXPERT_LANE0:EXPERT_LANE0 + ne].set(jnp.transpose(re_w, (1, 0, 2)).reshape(d, ne))
    br = jnp.zeros((1, ROUTER_LANES), F32)
    br = br.at[0, :N_GROUPS].set(rg_b)
    br = br.at[0, EXPERT_LANE0:EXPERT_LANE0 + ne].set(re_b.reshape(ne))
    return wr, br


def kernel(x_prompt, x_sample, c_prompt, c_sample, state_pool, cache_k, cache_v, ada_w, ada_b, ln_g, ln_b,
           pool_w, pool_scale, attn_w_qkv, attn_w_o, attn_rel_bias, router_group_w, router_group_b,
           router_expert_w, router_expert_b, moe_w_gate, moe_w_up, moe_w_down):
    d = D_MODEL
    n_p = _n_prompt()
    xp = x_prompt.reshape(n_p, d)
    xs = x_sample.reshape(_n_sample(), d)

    n_c = BATCH + DEC_BATCH
    c_rows = 32
    c_all = jnp.concatenate([c_prompt, c_sample, jnp.zeros((c_rows - n_c, d), F32)], axis=0)
    mods = _ada_params(c_all, ada_w, ada_b)
    sh1_0, sc1_0, g1_0, sh2_0, sc2_0, g2_0 = _split_mods(mods[0])
    sh1_1, sc1_1, g1_1, sh2_1, sc2_1, g2_1 = _split_mods(mods[1])

    expert_w = (moe_w_gate.astype(BF16), moe_w_up.astype(BF16), moe_w_down.astype(BF16))

    def router_w(l):
        return _router_weights(router_group_w[l], router_group_b[l], router_expert_w[l], router_expert_b[l])

    hist_pad = jnp.pad(state_pool[0], ((0, 0), (HALO - POOL_STATE, 0), (0, 0))).reshape(DEC_BATCH * HALO, d)
    pp, ps = _stack([sh1_0, sc1_0, g1_0, sh2_0, sc2_0])
    vecs = jnp.stack([pool_scale[0], ln_g[0, 0], ln_b[0, 0]])
    x1, t1, st_p, h_s = _pool_layer(xp, xs, hist_pad, pp, ps, pool_w[0].astype(BF16), vecs)
    pp, ps = _stack([g2_0, sh1_1, sc1_1])
    x2, a2 = _moe(t1, x1, 0, router_w(0), expert_w, pp, ps, jnp.stack([ln_g[0, 1], ln_b[0, 1]]), True)

    tiles_per_seq = SEQ // ROW_TILE
    keep_rows = min(N_LEFT_CHUNKS * CHUNK, SEQ)
    kt = keep_rows // ROW_TILE
    keep_tiles = [b * tiles_per_seq + tiles_per_seq - kt + r for b in range(BATCH) for r in range(kt)]
    keep_tiles.append(n_p // ROW_TILE)
    w_qkv = attn_w_qkv[0]
    w_q = (w_qkv[:, :d] * (HEAD_DIM ** -0.5 * LOG2_E)).astype(BF16)
    (q,) = _matmul(a2, w_q, BF16, name="attn_q")
    k, k_keep = _matmul(a2, w_qkv[:, d:2 * d].astype(BF16), BF16, keep_tiles, name="attn_k")
    v, v_keep = _matmul(a2, w_qkv[:, 2 * d:].astype(BF16), BF16, keep_tiles, name="attn_v")
    cache_rows = cache_k.shape[2]
    bias_p, bias_c, bias_n = _attn_biases(attn_rel_bias[0], cache_rows)
    o = _attn_prompt(q, k, v, bias_p)
    o = _attn_sample(q, k, v, cache_k[0].reshape(DEC_BATCH, cache_rows, d),
                     cache_v[0].reshape(DEC_BATCH, cache_rows, d), bias_c, bias_n, o)
    (m,) = _matmul(o, attn_w_o[0].astype(BF16), F32, name="attn_out")
    pp, ps = _stack([g1_1, sh2_1, sc2_1])
    x3, t3 = _ln1(x2, m, pp, ps, jnp.stack([ln_g[1, 0], ln_b[1, 0]]))
    pp, ps = _stack([g2_1])
    yp, ys = _moe(t3, x3, 1, router_w(1), expert_w, pp, ps, jnp.stack([ln_g[1, 1], ln_b[1, 1]]), False)

    n_heads = d // HEAD_DIM
    kp_rows = BATCH * keep_rows

    def prompt_cache(keep):
        return keep[:kp_rows].reshape(1, BATCH, keep_rows, n_heads, HEAD_DIM)

    def sample_cache(keep):
        return keep[kp_rows:kp_rows + _n_sample()].reshape(1, DEC_BATCH, DEC_SEQ, n_heads, HEAD_DIM)

    return (
        yp.reshape(BATCH, SEQ, d),
        ys.reshape(DEC_BATCH, DEC_SEQ, d),
        st_p[:, HALO - POOL_STATE:][None],
        prompt_cache(k_keep),
        prompt_cache(v_keep),
        h_s[:, HALO - POOL_STATE:][None],
        sample_cache(k_keep),
        sample_cache(v_keep),
    )
```

```python
import functools
import math

import jax
import jax.numpy as jnp
from jax import lax
from jax.experimental import pallas as pl
from jax.experimental.pallas import tpu as pltpu

D_MODEL = 4096
BATCH = 2
SEQ = 8192
DEPTH = 2
DEC_BATCH = 16
DEC_SEQ = 16
PAST_LEN = 4096
CHUNK = 64
POOL_WINDOWS = (2, 4, 8, 16)
POOL_STATE = 15
HEAD_DIM = 64
N_LEFT_CHUNKS = 8
REL_CLIP = 256
N_GROUPS = 4
EXPERTS_PER_GROUP = 8
D_EXPERT = 512
LN_EPS = 1e-5
NEG_INF = -1e30

ROW_TILE = 256
HALO = 16
EXPERT_TILE = 256
GMM_DOWN_COLS = 512
MM_TN = 1024
ATTN_Q = 256
ATTN_LANES = 512
ROUTER_LANES = 128
EXPERT_LANE0 = 32
VMEM_LIMIT = 56 * 1024 * 1024
LOG2_E = math.log2(math.e)

F32 = jnp.float32
BF16 = jnp.bfloat16


def _n_experts():
    return N_GROUPS * EXPERTS_PER_GROUP


def _n_prompt():
    return BATCH * SEQ


def _n_sample():
    return DEC_BATCH * DEC_SEQ


def _n_tot():
    return _n_prompt() + _n_sample()


def _max_expert_tiles():
    return (2 * _n_tot()) // EXPERT_TILE + _n_experts()


def _deepnorm_alpha():
    return (2 * DEPTH) ** 0.25


def _params(vmem=VMEM_LIMIT):
    return pltpu.CompilerParams(vmem_limit_bytes=vmem)


def _layer_norm(z, g, b):
    mu = jnp.mean(z, axis=-1, keepdims=True)
    zc = z - mu
    var = jnp.mean(zc * zc, axis=-1, keepdims=True)
    return zc * lax.rsqrt(var + LN_EPS) * g + b


def _ada_kernel(c_ref, w_ref, b_ref, o_ref):
    c = c_ref[...]
    s = (c / (1.0 + jnp.exp(-c))).astype(BF16)
    w = w_ref[0].astype(BF16)
    o_ref[0] = jnp.dot(s, w, preferred_element_type=F32) + b_ref[0]


def _ada_params(c_all, ada_w, ada_b):
    r, d = c_all.shape
    n = ada_w.shape[-1]
    tn = min(512, n)
    return pl.pallas_call(
        _ada_kernel,
        grid=(DEPTH, n // tn),
        in_specs=[
            pl.BlockSpec((r, d), lambda l, j: (0, 0)),
            pl.BlockSpec((1, d, tn), lambda l, j: (l, 0, j)),
            pl.BlockSpec((1, 1, tn), lambda l, j: (l, 0, j)),
        ],
        out_specs=pl.BlockSpec((1, r, tn), lambda l, j: (l, 0, j)),
        out_shape=jax.ShapeDtypeStruct((DEPTH, r, n), F32),
        compiler_params=_params(),
        name="ada_params",
    )(c_all, ada_w, ada_b.reshape(DEPTH, 1, n))


def _pool_kernel(x_ref, halo_ref, p_ref, w_ref, vec_ref, *rest, tm, tiles_per_seq, pos0, zero_first, aliased):
    if aliased:
        rest = rest[2:]
    x1_ref, t1_ref, st_ref, hbuf = rest
    i = pl.program_id(0)
    j = i % tiles_per_seq
    sh1 = p_ref[0, 0:1, :]
    sc1 = p_ref[0, 1:2, :]
    g1 = p_ref[0, 2:3, :]
    sh2 = p_ref[0, 3:4, :]
    sc2 = p_ref[0, 4:5, :]
    ls = vec_ref[0:1, :]
    ln_g = vec_ref[1:2, :]
    ln_b = vec_ref[2:3, :]

    x = x_ref[...]
    h = x * (1.0 + sc1) + sh1
    if zero_first:
        hh = halo_ref[...] * (1.0 + sc1) + sh1
        hh = jnp.where(j == 0, 0.0, hh)
    else:
        hh = halo_ref[...]
    hbuf[0:HALO, :] = hh
    hbuf[HALO:, :] = h

    pos = pos0 + j * tm + lax.broadcasted_iota(jnp.int32, (tm, 1), 0)
    gc = D_MODEL // len(POOL_WINDOWS)
    ys = []
    for g, w in enumerate(POOL_WINDOWS):
        s = hbuf[:, g * gc:(g + 1) * gc]
        k = 1
        while k < w:
            s = s + pltpu.roll(s, k, 0)
            k *= 2
        inv = 1.0 / jnp.minimum(pos + 1, w).astype(F32)
        pooled = s[HALO:, :] * inv - hbuf[HALO:, g * gc:(g + 1) * gc]
        ys.append(jnp.dot(pooled.astype(BF16), w_ref[g], preferred_element_type=F32))
    m = jnp.concatenate(ys, axis=-1) * ls
    x1 = _layer_norm(_deepnorm_alpha() * x + (1.0 + g1) * m, ln_g, ln_b)
    x1_ref[...] = x1
    t1_ref[...] = x1 * (1.0 + sc2) + sh2

    @pl.when(j == tiles_per_seq - 1)
    def _():
        st_ref[0] = hbuf[tm:, :]


def _pool_layer(xp, xs, hist_pad, pp, ps, pool_w, vecs):
    d = D_MODEL
    n_tot = _n_tot()
    gc = d // len(POOL_WINDOWS)
    np_tiles = _n_prompt() // ROW_TILE
    tps = SEQ // ROW_TILE
    hb = ROW_TILE // HALO
    common = [
        pl.BlockSpec((len(POOL_WINDOWS), gc, gc), lambda i: (0, 0, 0)),
        pl.BlockSpec((3, d), lambda i: (0, 0)),
    ]
    x1, t1, st_p = pl.pallas_call(
        functools.partial(_pool_kernel, tm=ROW_TILE, tiles_per_seq=tps, pos0=0, zero_first=True, aliased=False),
        grid=(np_tiles,),
        in_specs=[
            pl.BlockSpec((ROW_TILE, d), lambda i: (i, 0)),
            pl.BlockSpec((HALO, d), lambda i: (jnp.maximum(i * hb - 1, 0), 0)),
            pl.BlockSpec((1, 5, d), lambda i: (i // tps, 0, 0)),
        ] + common,
        out_specs=[
            pl.BlockSpec((ROW_TILE, d), lambda i: (i, 0)),
            pl.BlockSpec((ROW_TILE, d), lambda i: (i, 0)),
            pl.BlockSpec((1, HALO, d), lambda i: (i // tps, 0, 0)),
        ],
        out_shape=[
            jax.ShapeDtypeStruct((n_tot, d), F32),
            jax.ShapeDtypeStruct((n_tot, d), F32),
            jax.ShapeDtypeStruct((BATCH, HALO, d), F32),
        ],
        scratch_shapes=[pltpu.VMEM((ROW_TILE + HALO, d), F32)],
        compiler_params=_params(),
        name="pool_prompt",
    )(xp, xp, pp, pool_w, vecs)

    off = _n_prompt() // DEC_SEQ
    x1, t1, h_s = pl.pallas_call(
        functools.partial(_pool_kernel, tm=DEC_SEQ, tiles_per_seq=1, pos0=PAST_LEN, zero_first=False, aliased=True),
        grid=(DEC_BATCH,),
        in_specs=[
            pl.BlockSpec((DEC_SEQ, d), lambda i: (i, 0)),
            pl.BlockSpec((HALO, d), lambda i: (i, 0)),
            pl.BlockSpec((1, 5, d), lambda i: (i, 0, 0)),
        ] + common + [
            pl.BlockSpec(memory_space=pl.ANY),
            pl.BlockSpec(memory_space=pl.ANY),
        ],
        out_specs=[
            pl.BlockSpec((DEC_SEQ, d), lambda i: (off + i, 0)),
            pl.BlockSpec((DEC_SEQ, d), lambda i: (off + i, 0)),
            pl.BlockSpec((1, HALO, d), lambda i: (i, 0, 0)),
        ],
        out_shape=[
            jax.ShapeDtypeStruct((n_tot, d), F32),
            jax.ShapeDtypeStruct((n_tot, d), F32),
            jax.ShapeDtypeStruct((DEC_BATCH, HALO, d), F32),
        ],
        scratch_shapes=[pltpu.VMEM((DEC_SEQ + HALO, d), F32)],
        input_output_aliases={5: 0, 6: 1},
        compiler_params=_params(),
        name="pool_sample",
    )(xs, hist_pad, ps, pool_w, vecs, x1, t1)
    return x1, t1, st_p, h_s


def _router_kernel(t_ref, wr_ref, br_ref, carry_ref, *rest, tm, aliased):
    if aliased:
        rest = rest[2:]
    ri_ref, rw_ref, cnt_out_ref, cnt = rest
    i = pl.program_id(0)

    @pl.when(i == 0)
    def _():
        cnt[...] = carry_ref[...]

    logits = jnp.dot(t_ref[...], wr_ref[...], preferred_element_type=F32,
                     precision=lax.Precision.HIGHEST) + br_ref[...]
    lane = lax.broadcasted_iota(jnp.int32, (tm, ROUTER_LANES), 1).astype(F32)
    big = float(ROUTER_LANES)
    ninf = -jnp.inf

    lg = jnp.where(lane < N_GROUPS, logits, ninf)
    mg = jnp.max(lg, axis=-1, keepdims=True)
    gidx = jnp.min(jnp.where(lg == mg, lane, big), axis=-1, keepdims=True)
    p_group = 1.0 / jnp.sum(jnp.exp(lg - mg), axis=-1, keepdims=True)

    lo = EXPERT_LANE0 + gidx * EXPERTS_PER_GROUP
    le = jnp.where((lane >= lo) & (lane < lo + EXPERTS_PER_GROUP), logits, ninf)
    m1 = jnp.max(le, axis=-1, keepdims=True)
    i1 = jnp.min(jnp.where(le == m1, lane, big), axis=-1, keepdims=True)
    le2 = jnp.where(lane == i1, ninf, le)
    m2 = jnp.max(le2, axis=-1, keepdims=True)
    i2 = jnp.min(jnp.where(le2 == m2, lane, big), axis=-1, keepdims=True)
    e2 = jnp.exp(m2 - m1)
    w1 = p_group / (1.0 + e2)
    w2 = p_group * e2 / (1.0 + e2)

    hit1 = lane == i1
    hit2 = lane == i2
    onehot = jnp.where(hit1 | hit2, 1.0, 0.0)
    row = lax.broadcasted_iota(jnp.int32, (tm, tm), 0)
    col = lax.broadcasted_iota(jnp.int32, (tm, tm), 1)
    tri = jnp.where(col < row, 1.0, 0.0).astype(BF16)
    before = jnp.dot(tri, onehot.astype(BF16), preferred_element_type=F32) + cnt[...]
    r1 = jnp.sum(jnp.where(hit1, before, 0.0), axis=-1, keepdims=True)
    r2 = jnp.sum(jnp.where(hit2, before, 0.0), axis=-1, keepdims=True)
    cnt[...] = cnt[...] + jnp.sum(onehot, axis=0, keepdims=True)

    zero = jnp.zeros_like(logits)
    ints = jnp.where(lane == 0, i1 - EXPERT_LANE0,
                     jnp.where(lane == 1, i2 - EXPERT_LANE0,
                               jnp.where(lane == 2, r1, jnp.where(lane == 3, r2, zero))))
    ri_ref[...] = ints.astype(jnp.int32)
    rw_ref[...] = jnp.where(lane == 0, w1, jnp.where(lane == 1, w2, zero))
    cnt_out_ref[...] = cnt[...]


def _router(t, wr, br):
    d = D_MODEL
    n_tot = _n_tot()
    outs_shape = [
        jax.ShapeDtypeStruct((n_tot, ROUTER_LANES), jnp.int32),
        jax.ShapeDtypeStruct((n_tot, ROUTER_LANES), F32),
        jax.ShapeDtypeStruct((1, ROUTER_LANES), F32),
    ]
    fixed = [
        pl.BlockSpec((d, ROUTER_LANES), lambda i: (0, 0)),
        pl.BlockSpec((1, ROUTER_LANES), lambda i: (0, 0)),
        pl.BlockSpec((1, ROUTER_LANES), lambda i: (0, 0)),
    ]
    ri, rw, cnt = pl.pallas_call(
        functools.partial(_router_kernel, tm=ROW_TILE, aliased=False),
        grid=(_n_prompt() // ROW_TILE,),
        in_specs=[pl.BlockSpec((ROW_TILE, d), lambda i: (i, 0))] + fixed,
        out_specs=[
            pl.BlockSpec((ROW_TILE, ROUTER_LANES), lambda i: (i, 0)),
            pl.BlockSpec((ROW_TILE, ROUTER_LANES), lambda i: (i, 0)),
            pl.BlockSpec((1, ROUTER_LANES), lambda i: (0, 0)),
        ],
        out_shape=outs_shape,
        scratch_shapes=[pltpu.VMEM((1, ROUTER_LANES), F32)],
        compiler_params=_params(),
        name="router_prompt",
    )(t, wr, br, jnp.zeros((1, ROUTER_LANES), F32))
    off = _n_prompt() // DEC_SEQ
    ri, rw, cnt = pl.pallas_call(
        functools.partial(_router_kernel, tm=DEC_SEQ, aliased=True),
        grid=(DEC_BATCH,),
        in_specs=[pl.BlockSpec((DEC_SEQ, d), lambda i: (off + i, 0))] + fixed + [
            pl.BlockSpec(memory_space=pl.ANY),
            pl.BlockSpec(memory_space=pl.ANY),
        ],
        out_specs=[
            pl.BlockSpec((DEC_SEQ, ROUTER_LANES), lambda i: (off + i, 0)),
            pl.BlockSpec((DEC_SEQ, ROUTER_LANES), lambda i: (off + i, 0)),
            pl.BlockSpec((1, ROUTER_LANES), lambda i: (0, 0)),
        ],
        out_shape=outs_shape,
        scratch_shapes=[pltpu.VMEM((1, ROUTER_LANES), F32)],
        input_output_aliases={4: 0, 5: 1},
        compiler_params=_params(),
        name="router_sample",
    )(t, wr, br, cnt, ri, rw)
    return ri, rw, cnt


def _n_slots():
    return (_max_expert_tiles() + 2) * EXPERT_TILE


def _slot_kernel(pos_ref, code_ref):
    n_tot = _n_tot()
    n_pairs = 2 * n_tot

    def init(s, c):
        code_ref[s] = n_pairs + s
        return c

    def fill(p, c):
        code_ref[pos_ref[p]] = (p & 1) * n_tot + (p >> 1)
        return c

    lax.fori_loop(0, _n_slots(), init, 0, unroll=8)
    lax.fori_loop(0, n_pairs, fill, 0, unroll=8)


def _slot_codes(pos_flat):
    return pl.pallas_call(
        _slot_kernel,
        in_specs=[pl.BlockSpec(memory_space=pltpu.SMEM)],
        out_specs=pl.BlockSpec(memory_space=pltpu.SMEM),
        out_shape=jax.ShapeDtypeStruct((_n_slots(),), jnp.int32),
        name="moe_slots",
    )(pos_flat)


def _gmm_kernel(te_ref, nu_ref, code_ref, t_hbm, wg_ref, wu_ref, wd_ref, y_hbm,
                xbuf0, xbuf1, ybuf0, ybuf1, gsem, ssem):
    del te_ref
    i = pl.program_id(0)
    nu = nu_ref[0]
    tg = EXPERT_TILE
    warm0 = _max_expert_tiles()
    n_tot = _n_tot()
    xbufs = (xbuf0, xbuf1)
    ybufs = (ybuf0, ybuf1)

    def gather(tile, k, r):
        code = code_ref[tile * tg + r]
        src = jnp.minimum(jnp.where(code >= n_tot, code - n_tot, code), n_tot - 1)
        return pltpu.make_async_copy(t_hbm.at[pl.ds(src, 1)], xbufs[k].at[pl.ds(r, 1)], gsem.at[k])

    def scatter(tile, k, r):
        dst = code_ref[tile * tg + r]
        return pltpu.make_async_copy(ybufs[k].at[pl.ds(r, 1)], y_hbm.at[pl.ds(dst, 1)], ssem.at[k])

    def wait_gather(k):
        pltpu.make_async_copy(t_hbm.at[pl.ds(0, tg)], xbufs[k], gsem.at[k]).wait()

    def wait_scatter(k):
        pltpu.make_async_copy(ybufs[k], y_hbm.at[pl.ds(0, tg)], ssem.at[k]).wait()

    @pl.when(i == 0)
    def _():
        ybuf0[...] = jnp.zeros_like(ybuf0)
        ybuf1[...] = jnp.zeros_like(ybuf1)
        for r in range(tg):
            gather(0, 0, r).start()
        for r in range(tg):
            scatter(warm0, 0, r).start()

    def step(k):
        other = 1 - k
        nxt = jnp.minimum(i + 1, nu - 1)
        prev = jnp.where(i == 0, warm0 + 1, i - 1)
        for r in range(tg):
            gather(nxt, other, r).start()
        for r in range(tg):
            scatter(prev, other, r).start()
        wait_gather(k)
        wait_scatter(k)
        x = xbufs[k][...].astype(BF16)
        a = jnp.dot(x, wg_ref[0, 0], preferred_element_type=F32)
        b = jnp.dot(x, wu_ref[0, 0], preferred_element_type=F32)
        hid = ((a / (1.0 + jnp.exp(-a))) * b).astype(BF16)
        for c in range(D_MODEL // GMM_DOWN_COLS):
            cols = slice(c * GMM_DOWN_COLS, (c + 1) * GMM_DOWN_COLS)
            ybufs[k][:, cols] = jnp.dot(hid, wd_ref[0, 0, :, cols], preferred_element_type=F32)

        @pl.when(i == nu - 1)
        def _():
            for r in range(tg):
                scatter(i, k, r).start()
            wait_scatter(other)
            wait_scatter(k)
            wait_gather(other)

    for k in range(2):
        @pl.when((i < nu) & (i % 2 == k))
        def _(k=k):
            step(k)


def _gmm(t, code, tile_expert, n_used, layer, wg, wu, wd):
    d = D_MODEL
    tiles = _max_expert_tiles()

    def wsel(i, te, nu, code):
        return (layer, te[jnp.minimum(i, nu[0] - 1)], 0, 0)

    return pl.pallas_call(
        _gmm_kernel,
        grid_spec=pltpu.PrefetchScalarGridSpec(
            num_scalar_prefetch=3,
            grid=(tiles,),
            in_specs=[
                pl.BlockSpec(memory_space=pl.ANY),
                pl.BlockSpec((1, 1, d, D_EXPERT), wsel),
                pl.BlockSpec((1, 1, d, D_EXPERT), wsel),
                pl.BlockSpec((1, 1, D_EXPERT, d), wsel),
            ],
            out_specs=pl.BlockSpec(memory_space=pl.ANY),
            scratch_shapes=[
                pltpu.VMEM((EXPERT_TILE, d), F32),
                pltpu.VMEM((EXPERT_TILE, d), F32),
                pltpu.VMEM((EXPERT_TILE, d), F32),
                pltpu.VMEM((EXPERT_TILE, d), F32),
                pltpu.SemaphoreType.DMA((2,)),
                pltpu.SemaphoreType.DMA((2,)),
            ],
        ),
        out_shape=jax.ShapeDtypeStruct((2 * _n_tot() + _n_slots(), d), F32),
        compiler_params=_params(),
        name="moe_experts",
    )(tile_expert, n_used, code, t, wg, wu, wd)


def _post_kernel(x_ref, y0_ref, y1_ref, rw_ref, p_ref, vec_ref, *rest, emit_next, aliased):
    if aliased:
        rest = rest[(2 if emit_next else 1):]
    g2 = p_ref[0, 0:1, :]
    f = rw_ref[:, 0:1] * y0_ref[...] + rw_ref[:, 1:2] * y1_ref[...]
    xo = _layer_norm(_deepnorm_alpha() * x_ref[...] + (1.0 + g2) * f, vec_ref[0:1, :], vec_ref[1:2, :])
    rest[0][...] = xo
    if emit_next:
        rest[1][...] = (xo * (1.0 + p_ref[0, 2:3, :]) + p_ref[0, 1:2, :]).astype(BF16)


def _post(x, y, rw, pp, ps, vecs, emit_next):
    d = D_MODEL
    n_p, n_tot = _n_prompt(), _n_tot()
    np_tiles = n_p // ROW_TILE
    tps = SEQ // ROW_TILE
    off = n_p // DEC_SEQ
    npar = pp.shape[1]

    def call(tm, grid, roff, par, bmap, prev, name):
        aliased = prev is not None
        y1_off = n_tot // tm
        in_specs = [
            pl.BlockSpec((tm, d), lambda i: (roff + i, 0)),
            pl.BlockSpec((tm, d), lambda i: (roff + i, 0)),
            pl.BlockSpec((tm, d), lambda i: (y1_off + roff + i, 0)),
            pl.BlockSpec((tm, ROUTER_LANES), lambda i: (roff + i, 0)),
            pl.BlockSpec((1, npar, d), lambda i: (bmap(i), 0, 0)),
            pl.BlockSpec((2, d), lambda i: (0, 0)),
        ]
        args = [x, y, y, rw, par, vecs]
        if emit_next:
            oroff, rows = roff, n_tot
        else:
            oroff, rows = 0, grid * tm
        out_specs = [pl.BlockSpec((tm, d), lambda i: (oroff + i, 0))]
        out_shape = [jax.ShapeDtypeStruct((rows, d), F32)]
        if emit_next:
            out_specs.append(pl.BlockSpec((tm, d), lambda i: (oroff + i, 0)))
            out_shape.append(jax.ShapeDtypeStruct((rows, d), BF16))
        aliases = {}
        if aliased:
            for k, a in enumerate(prev):
                in_specs.append(pl.BlockSpec(memory_space=pl.ANY))
                args.append(a)
                aliases[6 + k] = k
        return pl.pallas_call(
            functools.partial(_post_kernel, emit_next=emit_next, aliased=aliased),
            grid=(grid,),
            in_specs=in_specs,
            out_specs=out_specs,
            out_shape=out_shape,
            input_output_aliases=aliases,
            compiler_params=_params(),
            name=name,
        )(*args)

    outs_p = call(ROW_TILE, np_tiles, 0, pp, lambda i: i // tps, None, "moe_post_prompt")
    if emit_next:
        return call(DEC_SEQ, DEC_BATCH, off, ps, lambda i: i, outs_p, "moe_post_sample")
    outs_s = call(DEC_SEQ, DEC_BATCH, off, ps, lambda i: i, None, "moe_post_sample")
    return outs_p[0], outs_s[0]


def _moe(t, x, layer, router_w, expert_w, pp, ps, vecs, emit_next):
    wr, br = router_w
    ri, rw, cnt = _router(t, wr, br)
    ne = _n_experts()
    counts = cnt[0, EXPERT_LANE0:EXPERT_LANE0 + ne].astype(jnp.int32)
    tiles_e = (counts + EXPERT_TILE - 1) // EXPERT_TILE
    tile_end = jnp.cumsum(tiles_e)
    pad_off = (tile_end - tiles_e) * EXPERT_TILE
    n_used = tile_end[-1:].astype(jnp.int32)
    tile_ids = jnp.arange(_max_expert_tiles(), dtype=jnp.int32)
    tile_expert = jnp.minimum(jnp.sum(tile_ids[:, None] >= tile_end[None, :], axis=1), ne - 1).astype(jnp.int32)
    eid = ri[:, 0:2]
    rank = ri[:, 2:4]
    onehot = eid[:, :, None] == jnp.arange(ne, dtype=jnp.int32)[None, None, :]
    pos = jnp.sum(jnp.where(onehot, pad_off[None, None, :], 0), axis=-1) + rank
    code = _slot_codes(pos.astype(jnp.int32).reshape(-1))
    y = _gmm(t, code, tile_expert, n_used, layer, *expert_w)
    return _post(x, y, rw, pp, ps, vecs, emit_next)


def _mm_kernel(a_ref, w_ref, *out_refs, keep_tiles):
    acc = jnp.dot(a_ref[...], w_ref[...], preferred_element_type=F32)
    out_refs[0][...] = acc.astype(out_refs[0].dtype)
    if keep_tiles:
        i = pl.program_id(1)
        keep = functools.reduce(jnp.logical_or, [i == k for k in keep_tiles])

        @pl.when(keep)
        def _():
            out_refs[1][...] = acc


def _matmul(a, w, out_dtype, keep_tiles=(), name="matmul"):
    m, k = a.shape
    n_out = w.shape[1]
    tm = ROW_TILE
    tn = min(MM_TN, n_out)
    out_specs = [pl.BlockSpec((tm, tn), lambda j, i: (i, j))]
    out_shape = [jax.ShapeDtypeStruct((m, n_out), out_dtype)]
    if keep_tiles:
        def slot(j, i):
            return (sum((i > b).astype(jnp.int32) for b in keep_tiles), j)

        out_specs.append(pl.BlockSpec((tm, tn), slot))
        out_shape.append(jax.ShapeDtypeStruct((len(keep_tiles) * tm, n_out), F32))
    outs = pl.pallas_call(
        functools.partial(_mm_kernel, keep_tiles=tuple(keep_tiles)),
        grid=(n_out // tn, m // tm),
        in_specs=[
            pl.BlockSpec((tm, k), lambda j, i: (i, 0)),
            pl.BlockSpec((k, tn), lambda j, i: (0, j)),
        ],
        out_specs=out_specs,
        out_shape=out_shape,
        compiler_params=_params(),
        name=name,
    )(a, w)
    return outs


def _softmax_pv(s_blocks, pens, v_blocks):
    m = None
    for s, pen in zip(s_blocks, pens):
        rm = jnp.max(s, axis=-1, keepdims=True) + pen
        m = rm if m is None else jnp.maximum(m, rm)
    o = None
    for s, pen, v in zip(s_blocks, pens, v_blocks):
        p = jnp.exp2(s - (m - pen))
        po = jnp.dot(p.astype(BF16), v, preferred_element_type=F32)
        o = po if o is None else o + po
    return o


def _merge_heads(first, o_a, o_b):
    o = jnp.where(first, o_a, o_b)
    den = jnp.where(first, pltpu.roll(o_a, HEAD_DIM, 1), pltpu.roll(o_b, HEAD_DIM, 1))
    return o / den


def _attn_prompt_kernel(q_ref, k0_ref, k1_ref, k2_ref, v0_ref, v1_ref, v2_ref, bias_ref, o_ref):
    cg = pl.program_id(2)
    k_refs = (k0_ref, k1_ref, k2_ref)
    v_refs = (v0_ref, v1_ref, v2_ref)
    pens = (jnp.where(cg >= 2, 0.0, NEG_INF), jnp.where(cg >= 1, 0.0, NEG_INF), 0.0)
    lane = lax.broadcasted_iota(jnp.int32, (ATTN_Q, 2 * HEAD_DIM), 1)
    first = lane < HEAD_DIM
    for p in range(ATTN_LANES // (2 * HEAD_DIM)):
        cols = slice(p * 2 * HEAD_DIM, (p + 1) * 2 * HEAD_DIM)
        q2 = q_ref[:, cols]
        v2 = [v_refs[kb][:, cols] for kb in range(3)]
        halves = []
        for hh in range(2):
            mine = first if hh == 0 else ~first
            qm = jnp.where(mine, q2, jnp.zeros_like(q2))
            s_blocks = []
            for kb in range(3):
                s = lax.dot_general(qm, k_refs[kb][:, cols], (((1,), (1,)), ((), ())),
                                    preferred_element_type=F32)
                s_blocks.append(s + bias_ref[2 * p + hh, :, kb * ATTN_Q:(kb + 1) * ATTN_Q])
            vm = [jnp.where(mine, v, jnp.ones_like(v)) for v in v2]
            halves.append(_softmax_pv(s_blocks, pens, vm))
        o_ref[:, cols] = _merge_heads(first, halves[0], halves[1]).astype(o_ref.dtype)


def _attn_prompt(q, k, v, bias):
    d = D_MODEL
    cgs = SEQ // ATTN_Q
    hpg = ATTN_LANES // HEAD_DIM

    def kmap(back):
        return lambda hg, b, c: (b * cgs + jnp.maximum(c - back, 0), hg)

    blk = (ATTN_Q, ATTN_LANES)
    return pl.pallas_call(
        _attn_prompt_kernel,
        grid=(d // ATTN_LANES, BATCH, cgs),
        in_specs=[
            pl.BlockSpec(blk, lambda hg, b, c: (b * cgs + c, hg)),
            pl.BlockSpec(blk, kmap(2)), pl.BlockSpec(blk, kmap(1)), pl.BlockSpec(blk, kmap(0)),
            pl.BlockSpec(blk, kmap(2)), pl.BlockSpec(blk, kmap(1)), pl.BlockSpec(blk, kmap(0)),
            pl.BlockSpec((hpg, ATTN_Q, 3 * ATTN_Q), lambda hg, b, c: (hg, 0, 0)),
        ],
        out_specs=pl.BlockSpec(blk, lambda hg, b, c: (b * cgs + c, hg)),
        out_shape=jax.ShapeDtypeStruct((_n_tot(), d), BF16),
        compiler_params=_params(),
        name="attn_prompt",
    )(q, k, k, k, v, v, v, bias)


def _attn_sample_kernel(q_ref, kn_ref, vn_ref, ck_ref, cv_ref, bc_ref, bn_ref, o_in_ref, o_ref):
    del o_in_ref
    lane = lax.broadcasted_iota(jnp.int32, (DEC_SEQ, 2 * HEAD_DIM), 1)
    first = lane < HEAD_DIM
    for p in range(ATTN_LANES // (2 * HEAD_DIM)):
        cols = slice(p * 2 * HEAD_DIM, (p + 1) * 2 * HEAD_DIM)
        q2 = q_ref[:, cols]
        kc = ck_ref[0, :, cols].astype(BF16)
        vc = cv_ref[0, :, cols].astype(BF16)
        kn = kn_ref[:, cols]
        vn = vn_ref[:, cols]
        halves = []
        for hh in range(2):
            mine = first if hh == 0 else ~first
            qm = jnp.where(mine, q2, jnp.zeros_like(q2))
            dn = (((1,), (1,)), ((), ()))
            s_c = lax.dot_general(qm, kc, dn, preferred_element_type=F32) + bc_ref[2 * p + hh]
            s_n = lax.dot_general(qm, kn, dn, preferred_element_type=F32) + bn_ref[2 * p + hh]
            vm = [jnp.where((lax.broadcasted_iota(jnp.int32, v.shape, 1) < HEAD_DIM) == (hh == 0), v,
                            jnp.ones_like(v)) for v in (vc, vn)]
            halves.append(_softmax_pv([s_c, s_n], (0.0, 0.0), vm))
        o_ref[:, cols] = _merge_heads(first, halves[0], halves[1]).astype(o_ref.dtype)


def _attn_sample(q, k, v, ck, cv, bias_c, bias_n, o):
    d = D_MODEL
    off = _n_prompt() // DEC_SEQ
    hpg = ATTN_LANES // HEAD_DIM
    rows = ck.shape[1]
    new = pl.BlockSpec((DEC_SEQ, ATTN_LANES), lambda hg, b: (off + b, hg))
    cache = pl.BlockSpec((1, rows, ATTN_LANES), lambda hg, b: (b, 0, hg))
    return pl.pallas_call(
        _attn_sample_kernel,
        grid=(d // ATTN_LANES, DEC_BATCH),
        in_specs=[
            new, new, new, cache, cache,
            pl.BlockSpec((hpg, DEC_SEQ, rows), lambda hg, b: (hg, 0, 0)),
            pl.BlockSpec((hpg, DEC_SEQ, DEC_SEQ), lambda hg, b: (hg, 0, 0)),
            pl.BlockSpec(memory_space=pl.ANY),
        ],
        out_specs=new,
        out_shape=jax.ShapeDtypeStruct((_n_tot(), d), BF16),
        input_output_aliases={7: 0},
        compiler_params=_params(),
        name="attn_sample",
    )(q, k, v, ck, cv, bias_c, bias_n, o)


def _rel_index(dist):
    return jnp.clip(dist, -REL_CLIP, REL_CLIP) + REL_CLIP


def _attn_biases(rel_bias, cache_rows):
    tbl = rel_bias * LOG2_E
    n_heads = tbl.shape[0]
    kw = 3 * ATTN_Q
    back = 2 * ATTN_Q
    span = kw + ATTN_Q - 1
    u = tbl[:, _rel_index(back + (ATTN_Q - 1) - jnp.arange(span))]
    rep = jnp.broadcast_to(jnp.pad(u, ((0, 0), (0, 1)))[:, None, :], (n_heads, ATTN_Q, span + 1))
    toep = rep.reshape(n_heads, ATTN_Q * (span + 1))[:, :ATTN_Q * span].reshape(n_heads, ATTN_Q, span)
    qi = jnp.arange(ATTN_Q)[:, None]
    kj = jnp.arange(kw)[None, :]
    qc = qi // CHUNK
    kc = kj // CHUNK - back // CHUNK
    vis = (kc <= qc) & (kc >= qc - N_LEFT_CHUNKS)
    bias_p = jnp.where(vis[None], toep[:, :, ATTN_Q - 1:], NEG_INF)
    si = jnp.arange(DEC_SEQ)[:, None]
    dist_c = si + cache_rows - jnp.arange(cache_rows)[None, :]
    dist_n = si - jnp.arange(DEC_SEQ)[None, :]
    return bias_p, tbl[:, _rel_index(dist_c)], tbl[:, _rel_index(dist_n)]


def _ln1_kernel(x_ref, m_ref, p_ref, vec_ref, *rest, aliased):
    if aliased:
        rest = rest[2:]
    x1_ref, t1_ref = rest
    g1 = p_ref[0, 0:1, :]
    x1 = _layer_norm(_deepnorm_alpha() * x_ref[...] + (1.0 + g1) * m_ref[...], vec_ref[0:1, :], vec_ref[1:2, :])
    x1_ref[...] = x1
    t1_ref[...] = x1 * (1.0 + p_ref[0, 2:3, :]) + p_ref[0, 1:2, :]


def _ln1(x, m, pp, ps, vecs):
    d = D_MODEL
    n_tot = _n_tot()
    tps = SEQ // ROW_TILE
    off = _n_prompt() // DEC_SEQ
    out_shape = [jax.ShapeDtypeStruct((n_tot, d), F32), jax.ShapeDtypeStruct((n_tot, d), F32)]

    def call(tm, grid, roff, par, bmap, prev, name):
        blk = pl.BlockSpec((tm, d), lambda i: (roff + i, 0))
        in_specs = [blk, blk, pl.BlockSpec((1, 3, d), lambda i: (bmap(i), 0, 0)), pl.BlockSpec((2, d), lambda i: (0, 0))]
        args = [x, m, par, vecs]
        aliases = {}
        if prev is not None:
            in_specs += [pl.BlockSpec(memory_space=pl.ANY)] * 2
            args += list(prev)
            aliases = {4: 0, 5: 1}
        return pl.pallas_call(
            functools.partial(_ln1_kernel, aliased=prev is not None),
            grid=(grid,),
            in_specs=in_specs,
            out_specs=[blk, blk],
            out_shape=out_shape,
            input_output_aliases=aliases,
            compiler_params=_params(),
            name=name,
        )(*args)

    outs = call(ROW_TILE, _n_prompt() // ROW_TILE, 0, pp, lambda i: i // tps, None, "attn_post_prompt")
    return call(DEC_SEQ, DEC_BATCH, off, ps, lambda i: i, outs, "attn_post_sample")


def _split_mods(mods_l):
    return jnp.split(mods_l, 6, axis=-1)


def _stack(vectors):
    st = jnp.stack(vectors, axis=1)
    return st[:BATCH], st[BATCH:BATCH + DEC_BATCH]


def _router_weights(rg_w, rg_b, re_w, re_b):
    d = D_MODEL
    ne = _n_experts()
    wr = jnp.zeros((d, ROUTER_LANES), F32)
    wr = wr.at[:, :N_GROUPS].set(rg_w)
    wr = wr.at[:, EXPERT_LANE0:EXPERT_LANE0 + ne].set(jnp.transpose(re_w, (1, 0, 2)).reshape(d, ne))
    br = jnp.zeros((1, ROUTER_LANES), F32)
    br = br.at[0, :N_GROUPS].set(rg_b)
    br = br.at[0, EXPERT_LANE0:EXPERT_LANE0 + ne].set(re_b.reshape(ne))
    return wr, br


def kernel(x_prompt, x_sample, c_prompt, c_sample, state_pool, cache_k, cache_v, ada_w, ada_b, ln_g, ln_b,
           pool_w, pool_scale, attn_w_qkv, attn_w_o, attn_rel_bias, router_group_w, router_group_b,
           router_expert_w, router_expert_b, moe_w_gate, moe_w_up, moe_w_down):
    d = D_MODEL
    n_p = _n_prompt()
    xp = x_prompt.reshape(n_p, d)
    xs = x_sample.reshape(_n_sample(), d)

    n_c = BATCH + DEC_BATCH
    c_rows = 32
    c_all = jnp.concatenate([c_prompt, c_sample, jnp.zeros((c_rows - n_c, d), F32)], axis=0)
    mods = _ada_params(c_all, ada_w, ada_b)
    sh1_0, sc1_0, g1_0, sh2_0, sc2_0, g2_0 = _split_mods(mods[0])
    sh1_1, sc1_1, g1_1, sh2_1, sc2_1, g2_1 = _split_mods(mods[1])

    expert_w = (moe_w_gate.astype(BF16), moe_w_up.astype(BF16), moe_w_down.astype(BF16))

    def router_w(l):
        return _router_weights(router_group_w[l], router_group_b[l], router_expert_w[l], router_expert_b[l])

    hist_pad = jnp.pad(state_pool[0], ((0, 0), (HALO - POOL_STATE, 0), (0, 0))).reshape(DEC_BATCH * HALO, d)
    pp, ps = _stack([sh1_0, sc1_0, g1_0, sh2_0, sc2_0])
    vecs = jnp.stack([pool_scale[0], ln_g[0, 0], ln_b[0, 0]])
    x1, t1, st_p, h_s = _pool_layer(xp, xs, hist_pad, pp, ps, pool_w[0].astype(BF16), vecs)
    pp, ps = _stack([g2_0, sh1_1, sc1_1])
    x2, a2 = _moe(t1, x1, 0, router_w(0), expert_w, pp, ps, jnp.stack([ln_g[0, 1], ln_b[0, 1]]), True)

    tiles_per_seq = SEQ // ROW_TILE
    keep_rows = min(N_LEFT_CHUNKS * CHUNK, SEQ)
    kt = keep_rows // ROW_TILE
    keep_tiles = [b * tiles_per_seq + tiles_per_seq - kt + r for b in range(BATCH) for r in range(kt)]
    keep_tiles.append(n_p // ROW_TILE)
    w_qkv = attn_w_qkv[0]
    w_q = (w_qkv[:, :d] * (HEAD_DIM ** -0.5 * LOG2_E)).astype(BF16)
    (q,) = _matmul(a2, w_q, BF16, name="attn_q")
    k, k_keep = _matmul(a2, w_qkv[:, d:2 * d].astype(BF16), BF16, keep_tiles, name="attn_k")
    v, v_keep = _matmul(a2, w_qkv[:, 2 * d:].astype(BF16), BF16, keep_tiles, name="attn_v")
    cache_rows = cache_k.shape[2]
    bias_p, bias_c, bias_n = _attn_biases(attn_rel_bias[0], cache_rows)
    o = _attn_prompt(q, k, v, bias_p)
    o = _attn_sample(q, k, v, cache_k[0].reshape(DEC_BATCH, cache_rows, d),
                     cache_v[0].reshape(DEC_BATCH, cache_rows, d), bias_c, bias_n, o)
    (m,) = _matmul(o, attn_w_o[0].astype(BF16), F32, name="attn_out")
    pp, ps = _stack([g1_1, sh2_1, sc2_1])
    x3, t3 = _ln1(x2, m, pp, ps, jnp.stack([ln_g[1, 0], ln_b[1, 0]]))
    pp, ps = _stack([g2_1])
    yp, ys = _moe(t3, x3, 1, router_w(1), expert_w, pp, ps, jnp.stack([ln_g[1, 1], ln_b[1, 1]]), False)

    n_heads = d // HEAD_DIM
    kp_rows = BATCH * keep_rows

    def prompt_cache(keep):
        return keep[:kp_rows].reshape(1, BATCH, keep_rows, n_heads, HEAD_DIM)

    def sample_cache(keep):
        return keep[kp_rows:kp_rows + _n_sample()].reshape(1, DEC_BATCH, DEC_SEQ, n_heads, HEAD_DIM)

    return (
        yp.reshape(BATCH, SEQ, d),
        ys.reshape(DEC_BATCH, DEC_SEQ, d),
        st_p[:, HALO - POOL_STATE:][None],
        prompt_cache(k_keep),
        prompt_cache(v_keep),
        h_s[:, HALO - POOL_STATE:][None],
        sample_cache(k_keep),
        sample_cache(v_keep),
    )
```

```python
import functools
import math

import jax
import jax.numpy as jnp
from jax import lax
from jax.experimental import pallas as pl
from jax.experimental.pallas import tpu as pltpu

D_MODEL = 4096
BATCH = 2
SEQ = 8192
DEPTH = 2
DEC_BATCH = 16
DEC_SEQ = 16
PAST_LEN = 4096
CHUNK = 64
POOL_WINDOWS = (2, 4, 8, 16)
POOL_STATE = 15
HEAD_DIM = 64
N_LEFT_CHUNKS = 8
REL_CLIP = 256
N_GROUPS = 4
EXPERTS_PER_GROUP = 8
D_EXPERT = 512
LN_EPS = 1e-5
NEG_INF = -1e30

ROW_TILE = 256
HALO = 16
EXPERT_TILE = 256
GMM_DOWN_COLS = 512
SCATTER_PRIORITY = 1
MM_TN = 1024
ATTN_Q = 256
ATTN_LANES = 512
ATTN_GROUP_PAIRS = 4
ROUTER_LANES = 128
EXPERT_LANE0 = 32
VMEM_LIMIT = 56 * 1024 * 1024
LOG2_E = math.log2(math.e)

F32 = jnp.float32
BF16 = jnp.bfloat16


def _n_experts():
    return N_GROUPS * EXPERTS_PER_GROUP


def _n_prompt():
    return BATCH * SEQ


def _n_sample():
    return DEC_BATCH * DEC_SEQ


def _n_tot():
    return _n_prompt() + _n_sample()


def _max_expert_tiles():
    return (2 * _n_tot()) // EXPERT_TILE + _n_experts()


def _deepnorm_alpha():
    return (2 * DEPTH) ** 0.25


def _params(vmem=VMEM_LIMIT):
    return pltpu.CompilerParams(vmem_limit_bytes=vmem)


def _layer_norm(z, g, b):
    mu = jnp.mean(z, axis=-1, keepdims=True)
    zc = z - mu
    var = jnp.mean(zc * zc, axis=-1, keepdims=True)
    return zc * lax.rsqrt(var + LN_EPS) * g + b


def _ada_kernel(c_ref, w_ref, b_ref, o_ref):
    c = c_ref[...]
    s = (c / (1.0 + jnp.exp(-c))).astype(BF16)
    w = w_ref[0].astype(BF16)
    o_ref[0] = jnp.dot(s, w, preferred_element_type=F32) + b_ref[0]


def _ada_params(c_all, ada_w, ada_b):
    r, d = c_all.shape
    n = ada_w.shape[-1]
    tn = min(512, n)
    return pl.pallas_call(
        _ada_kernel,
        grid=(DEPTH, n // tn),
        in_specs=[
            pl.BlockSpec((r, d), lambda l, j: (0, 0)),
            pl.BlockSpec((1, d, tn), lambda l, j: (l, 0, j)),
            pl.BlockSpec((1, 1, tn), lambda l, j: (l, 0, j)),
        ],
        out_specs=pl.BlockSpec((1, r, tn), lambda l, j: (l, 0, j)),
        out_shape=jax.ShapeDtypeStruct((DEPTH, r, n), F32),
        compiler_params=_params(),
        name="ada_params",
    )(c_all, ada_w, ada_b.reshape(DEPTH, 1, n))


def _pool_kernel(x_ref, halo_ref, p_ref, w_ref, vec_ref, *rest, tm, tiles_per_seq, pos0, zero_first, aliased):
    if aliased:
        rest = rest[2:]
    x1_ref, t1_ref, st_ref, hbuf = rest
    i = pl.program_id(0)
    j = i % tiles_per_seq
    sh1 = p_ref[0, 0:1, :]
    sc1 = p_ref[0, 1:2, :]
    g1 = p_ref[0, 2:3, :]
    sh2 = p_ref[0, 3:4, :]
    sc2 = p_ref[0, 4:5, :]
    ls = vec_ref[0:1, :]
    ln_g = vec_ref[1:2, :]
    ln_b = vec_ref[2:3, :]

    x = x_ref[...]
    h = x * (1.0 + sc1) + sh1
    if zero_first:
        hh = halo_ref[...] * (1.0 + sc1) + sh1
        hh = jnp.where(j == 0, 0.0, hh)
    else:
        hh = halo_ref[...]
    hbuf[0:HALO, :] = hh
    hbuf[HALO:, :] = h

    pos = pos0 + j * tm + lax.broadcasted_iota(jnp.int32, (tm, 1), 0)
    gc = D_MODEL // len(POOL_WINDOWS)
    ys = []
    for g, w in enumerate(POOL_WINDOWS):
        s = hbuf[:, g * gc:(g + 1) * gc]
        k = 1
        while k < w:
            s = s + pltpu.roll(s, k, 0)
            k *= 2
        inv = 1.0 / jnp.minimum(pos + 1, w).astype(F32)
        pooled = s[HALO:, :] * inv - hbuf[HALO:, g * gc:(g + 1) * gc]
        ys.append(jnp.dot(pooled.astype(BF16), w_ref[g], preferred_element_type=F32))
    m = jnp.concatenate(ys, axis=-1) * ls
    x1 = _layer_norm(_deepnorm_alpha() * x + (1.0 + g1) * m, ln_g, ln_b)
    x1_ref[...] = x1
    t1_ref[...] = x1 * (1.0 + sc2) + sh2

    @pl.when(j == tiles_per_seq - 1)
    def _():
        st_ref[0] = hbuf[tm:, :]


def _pool_layer(xp, xs, hist_pad, pp, ps, pool_w, vecs):
    d = D_MODEL
    n_tot = _n_tot()
    gc = d // len(POOL_WINDOWS)
    np_tiles = _n_prompt() // ROW_TILE
    tps = SEQ // ROW_TILE
    hb = ROW_TILE // HALO
    common = [
        pl.BlockSpec((len(POOL_WINDOWS), gc, gc), lambda i: (0, 0, 0)),
        pl.BlockSpec((3, d), lambda i: (0, 0)),
    ]
    x1, t1, st_p = pl.pallas_call(
        functools.partial(_pool_kernel, tm=ROW_TILE, tiles_per_seq=tps, pos0=0, zero_first=True, aliased=False),
        grid=(np_tiles,),
        in_specs=[
            pl.BlockSpec((ROW_TILE, d), lambda i: (i, 0)),
            pl.BlockSpec((HALO, d), lambda i: (jnp.maximum(i * hb - 1, 0), 0)),
            pl.BlockSpec((1, 5, d), lambda i: (i // tps, 0, 0)),
        ] + common,
        out_specs=[
            pl.BlockSpec((ROW_TILE, d), lambda i: (i, 0)),
            pl.BlockSpec((ROW_TILE, d), lambda i: (i, 0)),
            pl.BlockSpec((1, HALO, d), lambda i: (i // tps, 0, 0)),
        ],
        out_shape=[
            jax.ShapeDtypeStruct((n_tot, d), F32),
            jax.ShapeDtypeStruct((n_tot, d), F32),
            jax.ShapeDtypeStruct((BATCH, HALO, d), F32),
        ],
        scratch_shapes=[pltpu.VMEM((ROW_TILE + HALO, d), F32)],
        compiler_params=_params(),
        name="pool_prompt",
    )(xp, xp, pp, pool_w, vecs)

    off = _n_prompt() // DEC_SEQ
    x1, t1, h_s = pl.pallas_call(
        functools.partial(_pool_kernel, tm=DEC_SEQ, tiles_per_seq=1, pos0=PAST_LEN, zero_first=False, aliased=True),
        grid=(DEC_BATCH,),
        in_specs=[
            pl.BlockSpec((DEC_SEQ, d), lambda i: (i, 0)),
            pl.BlockSpec((HALO, d), lambda i: (i, 0)),
            pl.BlockSpec((1, 5, d), lambda i: (i, 0, 0)),
        ] + common + [
            pl.BlockSpec(memory_space=pl.ANY),
            pl.BlockSpec(memory_space=pl.ANY),
        ],
        out_specs=[
            pl.BlockSpec((DEC_SEQ, d), lambda i: (off + i, 0)),
            pl.BlockSpec((DEC_SEQ, d), lambda i: (off + i, 0)),
            pl.BlockSpec((1, HALO, d), lambda i: (i, 0, 0)),
        ],
        out_shape=[
            jax.ShapeDtypeStruct((n_tot, d), F32),
            jax.ShapeDtypeStruct((n_tot, d), F32),
            jax.ShapeDtypeStruct((DEC_BATCH, HALO, d), F32),
        ],
        scratch_shapes=[pltpu.VMEM((DEC_SEQ + HALO, d), F32)],
        input_output_aliases={5: 0, 6: 1},
        compiler_params=_params(),
        name="pool_sample",
    )(xs, hist_pad, ps, pool_w, vecs, x1, t1)
    return x1, t1, st_p, h_s


def _router_kernel(t_ref, wr_ref, br_ref, carry_ref, *rest, tm, aliased):
    if aliased:
        rest = rest[2:]
    ri_ref, rw_ref, cnt_out_ref, cnt = rest
    i = pl.program_id(0)

    @pl.when(i == 0)
    def _():
        cnt[...] = carry_ref[...]

    t = t_ref[...]
    t_hi = t.astype(BF16)
    t_lo = (t - t_hi.astype(F32)).astype(BF16)
    logits = (jnp.dot(t_hi, wr_ref[0], preferred_element_type=F32)
              + jnp.dot(t_lo, wr_ref[0], preferred_element_type=F32)
              + jnp.dot(t_hi, wr_ref[1], preferred_element_type=F32)) + br_ref[...]
    lane = lax.broadcasted_iota(jnp.int32, (tm, ROUTER_LANES), 1).astype(F32)
    big = float(ROUTER_LANES)
    ninf = -jnp.inf

    lg = jnp.where(lane < N_GROUPS, logits, ninf)
    mg = jnp.max(lg, axis=-1, keepdims=True)
    gidx = jnp.min(jnp.where(lg == mg, lane, big), axis=-1, keepdims=True)
    p_group = 1.0 / jnp.sum(jnp.exp(lg - mg), axis=-1, keepdims=True)

    lo = EXPERT_LANE0 + gidx * EXPERTS_PER_GROUP
    le = jnp.where((lane >= lo) & (lane < lo + EXPERTS_PER_GROUP), logits, ninf)
    m1 = jnp.max(le, axis=-1, keepdims=True)
    i1 = jnp.min(jnp.where(le == m1, lane, big), axis=-1, keepdims=True)
    le2 = jnp.where(lane == i1, ninf, le)
    m2 = jnp.max(le2, axis=-1, keepdims=True)
    i2 = jnp.min(jnp.where(le2 == m2, lane, big), axis=-1, keepdims=True)
    e2 = jnp.exp(m2 - m1)
    w1 = p_group / (1.0 + e2)
    w2 = p_group * e2 / (1.0 + e2)

    hit1 = lane == i1
    hit2 = lane == i2
    onehot = jnp.where(hit1 | hit2, 1.0, 0.0)
    row = lax.broadcasted_iota(jnp.int32, (tm, tm), 0)
    col = lax.broadcasted_iota(jnp.int32, (tm, tm), 1)
    tri = jnp.where(col < row, 1.0, 0.0).astype(BF16)
    before = jnp.dot(tri, onehot.astype(BF16), preferred_element_type=F32) + cnt[...]
    r1 = jnp.sum(jnp.where(hit1, before, 0.0), axis=-1, keepdims=True)
    r2 = jnp.sum(jnp.where(hit2, before, 0.0), axis=-1, keepdims=True)
    cnt[...] = cnt[...] + jnp.sum(onehot, axis=0, keepdims=True)

    zero = jnp.zeros_like(logits)
    ints = jnp.where(lane == 0, i1 - EXPERT_LANE0,
                     jnp.where(lane == 1, i2 - EXPERT_LANE0,
                               jnp.where(lane == 2, r1, jnp.where(lane == 3, r2, zero))))
    ri_ref[...] = ints.astype(jnp.int32)
    rw_ref[...] = jnp.where(lane == 0, w1, jnp.where(lane == 1, w2, zero))
    cnt_out_ref[...] = cnt[...]


def _router(t, wr, br):
    d = D_MODEL
    n_tot = _n_tot()
    outs_shape = [
        jax.ShapeDtypeStruct((n_tot, ROUTER_LANES), jnp.int32),
        jax.ShapeDtypeStruct((n_tot, ROUTER_LANES), F32),
        jax.ShapeDtypeStruct((1, ROUTER_LANES), F32),
    ]
    fixed = [
        pl.BlockSpec((2, d, ROUTER_LANES), lambda i: (0, 0, 0)),
        pl.BlockSpec((1, ROUTER_LANES), lambda i: (0, 0)),
        pl.BlockSpec((1, ROUTER_LANES), lambda i: (0, 0)),
    ]
    ri, rw, cnt = pl.pallas_call(
        functools.partial(_router_kernel, tm=ROW_TILE, aliased=False),
        grid=(_n_prompt() // ROW_TILE,),
        in_specs=[pl.BlockSpec((ROW_TILE, d), lambda i: (i, 0))] + fixed,
        out_specs=[
            pl.BlockSpec((ROW_TILE, ROUTER_LANES), lambda i: (i, 0)),
            pl.BlockSpec((ROW_TILE, ROUTER_LANES), lambda i: (i, 0)),
            pl.BlockSpec((1, ROUTER_LANES), lambda i: (0, 0)),
        ],
        out_shape=outs_shape,
        scratch_shapes=[pltpu.VMEM((1, ROUTER_LANES), F32)],
        compiler_params=_params(),
        name="router_prompt",
    )(t, wr, br, jnp.zeros((1, ROUTER_LANES), F32))
    off = _n_prompt() // DEC_SEQ
    ri, rw, cnt = pl.pallas_call(
        functools.partial(_router_kernel, tm=DEC_SEQ, aliased=True),
        grid=(DEC_BATCH,),
        in_specs=[pl.BlockSpec((DEC_SEQ, d), lambda i: (off + i, 0))] + fixed + [
            pl.BlockSpec(memory_space=pl.ANY),
            pl.BlockSpec(memory_space=pl.ANY),
        ],
        out_specs=[
            pl.BlockSpec((DEC_SEQ, ROUTER_LANES), lambda i: (off + i, 0)),
            pl.BlockSpec((DEC_SEQ, ROUTER_LANES), lambda i: (off + i, 0)),
            pl.BlockSpec((1, ROUTER_LANES), lambda i: (0, 0)),
        ],
        out_shape=outs_shape,
        scratch_shapes=[pltpu.VMEM((1, ROUTER_LANES), F32)],
        input_output_aliases={4: 0, 5: 1},
        compiler_params=_params(),
        name="router_sample",
    )(t, wr, br, cnt, ri, rw)
    return ri, rw, cnt


def _n_slots():
    return (_max_expert_tiles() + 2) * EXPERT_TILE


def _slot_kernel(pos_ref, code_ref):
    n_tot = _n_tot()
    n_pairs = 2 * n_tot

    def init(s, c):
        code_ref[s] = n_pairs + s
        return c

    def fill(r, c):
        code_ref[pos_ref[2 * r]] = r
        code_ref[pos_ref[2 * r + 1]] = n_tot + r
        return c

    lax.fori_loop(0, _n_slots(), init, 0, unroll=8)
    lax.fori_loop(0, n_tot, fill, 0, unroll=8)


def _slot_codes(pos_flat):
    return pl.pallas_call(
        _slot_kernel,
        in_specs=[pl.BlockSpec(memory_space=pltpu.SMEM)],
        out_specs=pl.BlockSpec(memory_space=pltpu.SMEM),
        out_shape=jax.ShapeDtypeStruct((_n_slots(),), jnp.int32),
        name="moe_slots",
    )(pos_flat)


def _gmm_kernel(te_ref, nu_ref, src_ref, code_ref, t_hbm, wg_ref, wu_ref, wd_ref, y_hbm,
                xbuf0, xbuf1, ybuf0, ybuf1, gsem, ssem):
    del te_ref
    i = pl.program_id(0)
    nu = nu_ref[0]
    tg = EXPERT_TILE
    warm0 = _max_expert_tiles()
    xbufs = (xbuf0, xbuf1)
    ybufs = (ybuf0, ybuf1)

    def gather(tile, k, r):
        src = src_ref[tile * tg + r]
        return pltpu.make_async_copy(t_hbm.at[pl.ds(src, 1)], xbufs[k].at[pl.ds(r, 1)], gsem.at[k])

    def scatter(tile, k, r):
        dst = code_ref[tile * tg + r]
        return pltpu.make_async_copy(ybufs[k].at[pl.ds(r, 1)], y_hbm.at[pl.ds(dst, 1)], ssem.at[k])

    def wait_gather(k):
        pltpu.make_async_copy(t_hbm.at[pl.ds(0, tg)], xbufs[k], gsem.at[k]).wait()

    def wait_scatter(k):
        pltpu.make_async_copy(ybufs[k], y_hbm.at[pl.ds(0, tg)], ssem.at[k]).wait()

    @pl.when(i == 0)
    def _():
        ybuf0[...] = jnp.zeros_like(ybuf0)
        ybuf1[...] = jnp.zeros_like(ybuf1)
        for r in range(tg):
            gather(0, 0, r).start()
        for r in range(tg):
            scatter(warm0, 0, r).start(priority=SCATTER_PRIORITY)

    def step(k):
        other = 1 - k
        nxt = jnp.minimum(i + 1, nu - 1)
        prev = jnp.where(i == 0, warm0 + 1, i - 1)
        for r in range(tg):
            gather(nxt, other, r).start()
        for r in range(tg):
            scatter(prev, other, r).start(priority=SCATTER_PRIORITY)
        wait_gather(k)
        wait_scatter(k)
        x = xbufs[k][...].astype(BF16)
        a = jnp.dot(x, wg_ref[0, 0], preferred_element_type=F32)
        b = jnp.dot(x, wu_ref[0, 0], preferred_element_type=F32)
        hid = ((a / (1.0 + jnp.exp(-a))) * b).astype(BF16)
        for c in range(D_MODEL // GMM_DOWN_COLS):
            cols = slice(c * GMM_DOWN_COLS, (c + 1) * GMM_DOWN_COLS)
            ybufs[k][:, cols] = jnp.dot(hid, wd_ref[0, 0, :, cols], preferred_element_type=F32)

        @pl.when(i == nu - 1)
        def _():
            for r in range(tg):
                scatter(i, k, r).start(priority=SCATTER_PRIORITY)
            wait_scatter(other)
            wait_scatter(k)
            wait_gather(other)

    for k in range(2):
        @pl.when((i < nu) & (i % 2 == k))
        def _(k=k):
            step(k)


def _gmm(t, code, tile_expert, n_used, layer, wg, wu, wd):
    d = D_MODEL
    tiles = _max_expert_tiles()
    n_tot = _n_tot()
    src = jnp.minimum(jnp.where(code >= n_tot, code - n_tot, code), n_tot - 1)

    def wsel(i, te, nu, src, code):
        return (layer, te[jnp.minimum(i, nu[0] - 1)], 0, 0)

    return pl.pallas_call(
        _gmm_kernel,
        grid_spec=pltpu.PrefetchScalarGridSpec(
            num_scalar_prefetch=4,
            grid=(tiles,),
            in_specs=[
                pl.BlockSpec(memory_space=pl.ANY),
                pl.BlockSpec((1, 1, d, D_EXPERT), wsel),
                pl.BlockSpec((1, 1, d, D_EXPERT), wsel),
                pl.BlockSpec((1, 1, D_EXPERT, d), wsel),
            ],
            out_specs=pl.BlockSpec(memory_space=pl.ANY),
            scratch_shapes=[
                pltpu.VMEM((EXPERT_TILE, d), F32),
                pltpu.VMEM((EXPERT_TILE, d), F32),
                pltpu.VMEM((EXPERT_TILE, d), F32),
                pltpu.VMEM((EXPERT_TILE, d), F32),
                pltpu.SemaphoreType.DMA((2,)),
                pltpu.SemaphoreType.DMA((2,)),
            ],
        ),
        out_shape=jax.ShapeDtypeStruct((2 * _n_tot() + _n_slots(), d), F32),
        compiler_params=_params(),
        name="moe_experts",
    )(tile_expert, n_used, src, code, t, wg, wu, wd)


def _post_kernel(x_ref, y0_ref, y1_ref, rw_ref, p_ref, vec_ref, *rest, emit_next, aliased):
    if aliased:
        rest = rest[(2 if emit_next else 1):]
    g2 = p_ref[0, 0:1, :]
    f = rw_ref[:, 0:1] * y0_ref[...] + rw_ref[:, 1:2] * y1_ref[...]
    xo = _layer_norm(_deepnorm_alpha() * x_ref[...] + (1.0 + g2) * f, vec_ref[0:1, :], vec_ref[1:2, :])
    rest[0][...] = xo
    if emit_next:
        rest[1][...] = (xo * (1.0 + p_ref[0, 2:3, :]) + p_ref[0, 1:2, :]).astype(BF16)


def _post(x, y, rw, pp, ps, vecs, emit_next):
    d = D_MODEL
    n_p, n_tot = _n_prompt(), _n_tot()
    np_tiles = n_p // ROW_TILE
    tps = SEQ // ROW_TILE
    off = n_p // DEC_SEQ
    npar = pp.shape[1]

    def call(tm, grid, roff, par, bmap, prev, name):
        aliased = prev is not None
        y1_off = n_tot // tm
        in_specs = [
            pl.BlockSpec((tm, d), lambda i: (roff + i, 0)),
            pl.BlockSpec((tm, d), lambda i: (roff + i, 0)),
            pl.BlockSpec((tm, d), lambda i: (y1_off + roff + i, 0)),
            pl.BlockSpec((tm, ROUTER_LANES), lambda i: (roff + i, 0)),
            pl.BlockSpec((1, npar, d), lambda i: (bmap(i), 0, 0)),
            pl.BlockSpec((2, d), lambda i: (0, 0)),
        ]
        args = [x, y, y, rw, par, vecs]
        if emit_next:
            oroff, rows = roff, n_tot
        else:
            oroff, rows = 0, grid * tm
        out_specs = [pl.BlockSpec((tm, d), lambda i: (oroff + i, 0))]
        out_shape = [jax.ShapeDtypeStruct((rows, d), F32)]
        if emit_next:
            out_specs.append(pl.BlockSpec((tm, d), lambda i: (oroff + i, 0)))
            out_shape.append(jax.ShapeDtypeStruct((rows, d), BF16))
        aliases = {}
        if aliased:
            for k, a in enumerate(prev):
                in_specs.append(pl.BlockSpec(memory_space=pl.ANY))
                args.append(a)
                aliases[6 + k] = k
        return pl.pallas_call(
            functools.partial(_post_kernel, emit_next=emit_next, aliased=aliased),
            grid=(grid,),
            in_specs=in_specs,
            out_specs=out_specs,
            out_shape=out_shape,
            input_output_aliases=aliases,
            compiler_params=_params(),
            name=name,
        )(*args)

    outs_p = call(ROW_TILE, np_tiles, 0, pp, lambda i: i // tps, None, "moe_post_prompt")
    if emit_next:
        return call(DEC_SEQ, DEC_BATCH, off, ps, lambda i: i, outs_p, "moe_post_sample")
    outs_s = call(DEC_SEQ, DEC_BATCH, off, ps, lambda i: i, None, "moe_post_sample")
    return outs_p[0], outs_s[0]


def _moe(t, x, layer, router_w, expert_w, pp, ps, vecs, emit_next):
    wr, br = router_w
    ri, rw, cnt = _router(t, wr, br)
    ne = _n_experts()
    counts = cnt[0, EXPERT_LANE0:EXPERT_LANE0 + ne].astype(jnp.int32)
    tiles_e = (counts + EXPERT_TILE - 1) // EXPERT_TILE
    tile_end = jnp.cumsum(tiles_e)
    pad_off = (tile_end - tiles_e) * EXPERT_TILE
    n_used = tile_end[-1:].astype(jnp.int32)
    tile_ids = jnp.arange(_max_expert_tiles(), dtype=jnp.int32)
    tile_expert = jnp.minimum(jnp.sum(tile_ids[:, None] >= tile_end[None, :], axis=1), ne - 1).astype(jnp.int32)
    eid = ri[:, 0:2]
    rank = ri[:, 2:4]
    onehot = eid[:, :, None] == jnp.arange(ne, dtype=jnp.int32)[None, None, :]
    pos = jnp.sum(jnp.where(onehot, pad_off[None, None, :], 0), axis=-1) + rank
    code = _slot_codes(pos.astype(jnp.int32).reshape(-1))
    y = _gmm(t, code, tile_expert, n_used, layer, *expert_w)
    return _post(x, y, rw, pp, ps, vecs, emit_next)


def _mm_kernel(a_ref, w_ref, *out_refs, keep_tiles):
    acc = jnp.dot(a_ref[...], w_ref[...], preferred_element_type=F32)
    out_refs[0][...] = acc.astype(out_refs[0].dtype)
    if keep_tiles:
        i = pl.program_id(1)
        keep = functools.reduce(jnp.logical_or, [i == k for k in keep_tiles])

        @pl.when(keep)
        def _():
            out_refs[1][...] = acc


def _matmul(a, w, out_dtype, keep_tiles=(), name="matmul"):
    m, k = a.shape
    n_out = w.shape[1]
    tm = ROW_TILE
    tn = min(MM_TN, n_out)
    out_specs = [pl.BlockSpec((tm, tn), lambda j, i: (i, j))]
    out_shape = [jax.ShapeDtypeStruct((m, n_out), out_dtype)]
    if keep_tiles:
        def slot(j, i):
            return (sum((i > b).astype(jnp.int32) for b in keep_tiles), j)

        out_specs.append(pl.BlockSpec((tm, tn), slot))
        out_shape.append(jax.ShapeDtypeStruct((len(keep_tiles) * tm, n_out), F32))
    outs = pl.pallas_call(
        functools.partial(_mm_kernel, keep_tiles=tuple(keep_tiles)),
        grid=(n_out // tn, m // tm),
        in_specs=[
            pl.BlockSpec((tm, k), lambda j, i: (i, 0)),
            pl.BlockSpec((k, tn), lambda j, i: (0, j)),
        ],
        out_specs=out_specs,
        out_shape=out_shape,
        compiler_params=_params(),
        name=name,
    )(a, w)
    return outs


def _softmax_pv(s_blocks, pens, v_blocks):
    m = None
    for s, pen in zip(s_blocks, pens):
        rm = jnp.max(s, axis=-1, keepdims=True) + pen
        m = rm if m is None else jnp.maximum(m, rm)
    o = None
    for s, pen, v in zip(s_blocks, pens, v_blocks):
        p = jnp.exp2(s - (m - pen))
        po = jnp.dot(p.astype(BF16), v, preferred_element_type=F32)
        o = po if o is None else o + po
    return o


def _merge_heads(first, o_a, o_b):
    o = jnp.where(first, o_a, o_b)
    den = jnp.where(first, pltpu.roll(o_a, HEAD_DIM, 1), pltpu.roll(o_b, HEAD_DIM, 1))
    return o / den


def _attn_prompt_kernel(q_ref, k0_ref, k1_ref, k2_ref, v0_ref, v1_ref, v2_ref, bias_ref, o_ref):
    cg = pl.program_id(2)
    k_refs = (k0_ref, k1_ref, k2_ref)
    v_refs = (v0_ref, v1_ref, v2_ref)
    pens = (jnp.where(cg >= 2, 0.0, NEG_INF), jnp.where(cg >= 1, 0.0, NEG_INF), 0.0)
    lane = lax.broadcasted_iota(jnp.int32, (ATTN_Q, 2 * HEAD_DIM), 1)
    first = lane < HEAD_DIM
    n_pairs = ATTN_LANES // (2 * HEAD_DIM)

    def cols(p):
        return slice(p * 2 * HEAD_DIM, (p + 1) * 2 * HEAD_DIM)

    for p0 in range(0, n_pairs, ATTN_GROUP_PAIRS):
        heads = [(p, hh) for p in range(p0, p0 + ATTN_GROUP_PAIRS) for hh in range(2)]
        scores = {}
        tops = {}
        for p, hh in heads:
            q2 = q_ref[:, cols(p)]
            qm = jnp.where(first if hh == 0 else ~first, q2, jnp.zeros_like(q2))
            scores[p, hh] = [
                lax.dot_general(qm, k_refs[kb][:, cols(p)], (((1,), (1,)), ((), ())), preferred_element_type=F32)
                + bias_ref[2 * p + hh, :, kb * ATTN_Q:(kb + 1) * ATTN_Q] for kb in range(3)]
            m = None
            for s, pen in zip(scores[p, hh], pens):
                rm = jnp.max(s, axis=-1, keepdims=True) + pen
                m = rm if m is None else jnp.maximum(m, rm)
            tops[p, hh] = m
        outs = {}
        for p, hh in heads:
            o = None
            for kb in range(3):
                v = v_refs[kb][:, cols(p)]
                vm = jnp.where(first if hh == 0 else ~first, v, jnp.ones_like(v))
                pr = jnp.exp2(scores[p, hh][kb] - (tops[p, hh] - pens[kb]))
                po = jnp.dot(pr.astype(BF16), vm, preferred_element_type=F32)
                o = po if o is None else o + po
            outs[p, hh] = o
        for p in range(p0, p0 + ATTN_GROUP_PAIRS):
            o_ref[:, cols(p)] = _merge_heads(first, outs[p, 0], outs[p, 1]).astype(o_ref.dtype)


def _attn_prompt(q, k, v, bias):
    d = D_MODEL
    cgs = SEQ // ATTN_Q
    hpg = ATTN_LANES // HEAD_DIM

    def kmap(back):
        return lambda hg, b, c: (b * cgs + jnp.maximum(c - back, 0), hg)

    blk = (ATTN_Q, ATTN_LANES)
    return pl.pallas_call(
        _attn_prompt_kernel,
        grid=(d // ATTN_LANES, BATCH, cgs),
        in_specs=[
            pl.BlockSpec(blk, lambda hg, b, c: (b * cgs + c, hg)),
            pl.BlockSpec(blk, kmap(2)), pl.BlockSpec(blk, kmap(1)), pl.BlockSpec(blk, kmap(0)),
            pl.BlockSpec(blk, kmap(2)), pl.BlockSpec(blk, kmap(1)), pl.BlockSpec(blk, kmap(0)),
            pl.BlockSpec((hpg, ATTN_Q, 3 * ATTN_Q), lambda hg, b, c: (hg, 0, 0)),
        ],
        out_specs=pl.BlockSpec(blk, lambda hg, b, c: (b * cgs + c, hg)),
        out_shape=jax.ShapeDtypeStruct((_n_tot(), d), BF16),
        compiler_params=_params(),
        name="attn_prompt",
    )(q, k, k, k, v, v, v, bias)


def _attn_sample_kernel(q_ref, kn_ref, vn_ref, ck_ref, cv_ref, bc_ref, bn_ref, o_in_ref, o_ref):
    del o_in_ref
    lane = lax.broadcasted_iota(jnp.int32, (DEC_SEQ, 2 * HEAD_DIM), 1)
    first = lane < HEAD_DIM
    for p in range(ATTN_LANES // (2 * HEAD_DIM)):
        cols = slice(p * 2 * HEAD_DIM, (p + 1) * 2 * HEAD_DIM)
        q2 = q_ref[:, cols]
        kc = ck_ref[0, :, cols].astype(BF16)
        vc = cv_ref[0, :, cols].astype(BF16)
        kn = kn_ref[:, cols]
        vn = vn_ref[:, cols]
        halves = []
        for hh in range(2):
            mine = first if hh == 0 else ~first
            qm = jnp.where(mine, q2, jnp.zeros_like(q2))
            dn = (((1,), (1,)), ((), ()))
            s_c = lax.dot_general(qm, kc, dn, preferred_element_type=F32) + bc_ref[2 * p + hh]
            s_n = lax.dot_general(qm, kn, dn, preferred_element_type=F32) + bn_ref[2 * p + hh]
            vm = [jnp.where((lax.broadcasted_iota(jnp.int32, v.shape, 1) < HEAD_DIM) == (hh == 0), v,
                            jnp.ones_like(v)) for v in (vc, vn)]
            halves.append(_softmax_pv([s_c, s_n], (0.0, 0.0), vm))
        o_ref[:, cols] = _merge_heads(first, halves[0], halves[1]).astype(o_ref.dtype)


def _attn_sample(q, k, v, ck, cv, bias_c, bias_n, o):
    d = D_MODEL
    off = _n_prompt() // DEC_SEQ
    hpg = ATTN_LANES // HEAD_DIM
    rows = ck.shape[1]
    new = pl.BlockSpec((DEC_SEQ, ATTN_LANES), lambda hg, b: (off + b, hg))
    cache = pl.BlockSpec((1, rows, ATTN_LANES), lambda hg, b: (b, 0, hg))
    return pl.pallas_call(
        _attn_sample_kernel,
        grid=(d // ATTN_LANES, DEC_BATCH),
        in_specs=[
            new, new, new, cache, cache,
            pl.BlockSpec((hpg, DEC_SEQ, rows), lambda hg, b: (hg, 0, 0)),
            pl.BlockSpec((hpg, DEC_SEQ, DEC_SEQ), lambda hg, b: (hg, 0, 0)),
            pl.BlockSpec(memory_space=pl.ANY),
        ],
        out_specs=new,
        out_shape=jax.ShapeDtypeStruct((_n_tot(), d), BF16),
        input_output_aliases={7: 0},
        compiler_params=_params(),
        name="attn_sample",
    )(q, k, v, ck, cv, bias_c, bias_n, o)


def _rel_index(dist):
    return jnp.clip(dist, -REL_CLIP, REL_CLIP) + REL_CLIP


def _attn_biases(rel_bias, cache_rows):
    tbl = rel_bias * LOG2_E
    n_heads = tbl.shape[0]
    kw = 3 * ATTN_Q
    back = 2 * ATTN_Q
    span = kw + ATTN_Q - 1
    u = tbl[:, _rel_index(back + (ATTN_Q - 1) - jnp.arange(span))]
    rep = jnp.broadcast_to(jnp.pad(u, ((0, 0), (0, 1)))[:, None, :], (n_heads, ATTN_Q, span + 1))
    toep = rep.reshape(n_heads, ATTN_Q * (span + 1))[:, :ATTN_Q * span].reshape(n_heads, ATTN_Q, span)
    qi = jnp.arange(ATTN_Q)[:, None]
    kj = jnp.arange(kw)[None, :]
    qc = qi // CHUNK
    kc = kj // CHUNK - back // CHUNK
    vis = (kc <= qc) & (kc >= qc - N_LEFT_CHUNKS)
    bias_p = jnp.where(vis[None], toep[:, :, ATTN_Q - 1:], NEG_INF)
    si = jnp.arange(DEC_SEQ)[:, None]
    dist_c = si + cache_rows - jnp.arange(cache_rows)[None, :]
    dist_n = si - jnp.arange(DEC_SEQ)[None, :]
    return bias_p, tbl[:, _rel_index(dist_c)], tbl[:, _rel_index(dist_n)]


def _ln1_kernel(x_ref, m_ref, p_ref, vec_ref, *rest, aliased):
    if aliased:
        rest = rest[2:]
    x1_ref, t1_ref = rest
    g1 = p_ref[0, 0:1, :]
    x1 = _layer_norm(_deepnorm_alpha() * x_ref[...] + (1.0 + g1) * m_ref[...], vec_ref[0:1, :], vec_ref[1:2, :])
    x1_ref[...] = x1
    t1_ref[...] = x1 * (1.0 + p_ref[0, 2:3, :]) + p_ref[0, 1:2, :]


def _ln1(x, m, pp, ps, vecs):
    d = D_MODEL
    n_tot = _n_tot()
    tps = SEQ // ROW_TILE
    off = _n_prompt() // DEC_SEQ
    out_shape = [jax.ShapeDtypeStruct((n_tot, d), F32), jax.ShapeDtypeStruct((n_tot, d), F32)]

    def call(tm, grid, roff, par, bmap, prev, name):
        blk = pl.BlockSpec((tm, d), lambda i: (roff + i, 0))
        in_specs = [blk, blk, pl.BlockSpec((1, 3, d), lambda i: (bmap(i), 0, 0)), pl.BlockSpec((2, d), lambda i: (0, 0))]
        args = [x, m, par, vecs]
        aliases = {}
        if prev is not None:
            in_specs += [pl.BlockSpec(memory_space=pl.ANY)] * 2
            args += list(prev)
            aliases = {4: 0, 5: 1}
        return pl.pallas_call(
            functools.partial(_ln1_kernel, aliased=prev is not None),
            grid=(grid,),
            in_specs=in_specs,
            out_specs=[blk, blk],
            out_shape=out_shape,
            input_output_aliases=aliases,
            compiler_params=_params(),
            name=name,
        )(*args)

    outs = call(ROW_TILE, _n_prompt() // ROW_TILE, 0, pp, lambda i: i // tps, None, "attn_post_prompt")
    return call(DEC_SEQ, DEC_BATCH, off, ps, lambda i: i, outs, "attn_post_sample")


def _split_mods(mods_l):
    return jnp.split(mods_l, 6, axis=-1)


def _stack(vectors):
    st = jnp.stack(vectors, axis=1)
    return st[:BATCH], st[BATCH:BATCH + DEC_BATCH]


def _router_weights(rg_w, rg_b, re_w, re_b):
    d = D_MODEL
    ne = _n_experts()
    wr = jnp.zeros((d, ROUTER_LANES), F32)
    wr = wr.at[:, :N_GROUPS].set(rg_w)
    wr = wr.at[:, EXPERT_LANE0:EXPERT_LANE0 + ne].set(jnp.transpose(re_w, (1, 0, 2)).reshape(d, ne))
    br = jnp.zeros((1, ROUTER_LANES), F32)
    br = br.at[0, :N_GROUPS].set(rg_b)
    br = br.at[0, EXPERT_LANE0:EXPERT_LANE0 + ne].set(re_b.reshape(ne))
    wr_hi = wr.astype(BF16)
    wr_lo = (wr - wr_hi.astype(F32)).astype(BF16)
    return jnp.stack([wr_hi, wr_lo]), br


def kernel(x_prompt, x_sample, c_prompt, c_sample, state_pool, cache_k, cache_v, ada_w, ada_b, ln_g, ln_b,
           pool_w, pool_scale, attn_w_qkv, attn_w_o, attn_rel_bias, router_group_w, router_group_b,
           router_expert_w, router_expert_b, moe_w_gate, moe_w_up, moe_w_down):
    d = D_MODEL
    n_p = _n_prompt()
    xp = x_prompt.reshape(n_p, d)
    xs = x_sample.reshape(_n_sample(), d)

    n_c = BATCH + DEC_BATCH
    c_rows = 32
    c_all = jnp.concatenate([c_prompt, c_sample, jnp.zeros((c_rows - n_c, d), F32)], axis=0)
    mods = _ada_params(c_all, ada_w, ada_b)
    sh1_0, sc1_0, g1_0, sh2_0, sc2_0, g2_0 = _split_mods(mods[0])
    sh1_1, sc1_1, g1_1, sh2_1, sc2_1, g2_1 = _split_mods(mods[1])

    expert_w = (moe_w_gate.astype(BF16), moe_w_up.astype(BF16), moe_w_down.astype(BF16))

    def router_w(l):
        return _router_weights(router_group_w[l], router_group_b[l], router_expert_w[l], router_expert_b[l])

    hist_pad = jnp.pad(state_pool[0], ((0, 0), (HALO - POOL_STATE, 0), (0, 0))).reshape(DEC_BATCH * HALO, d)
    pp, ps = _stack([sh1_0, sc1_0, g1_0, sh2_0, sc2_0])
    vecs = jnp.stack([pool_scale[0], ln_g[0, 0], ln_b[0, 0]])
    x1, t1, st_p, h_s = _pool_layer(xp, xs, hist_pad, pp, ps, pool_w[0].astype(BF16), vecs)
    pp, ps = _stack([g2_0, sh1_1, sc1_1])
    x2, a2 = _moe(t1, x1, 0, router_w(0), expert_w, pp, ps, jnp.stack([ln_g[0, 1], ln_b[0, 1]]), True)

    tiles_per_seq = SEQ // ROW_TILE
    keep_rows = min(N_LEFT_CHUNKS * CHUNK, SEQ)
    kt = keep_rows // ROW_TILE
    keep_tiles = [b * tiles_per_seq + tiles_per_seq - kt + r for b in range(BATCH) for r in range(kt)]
    keep_tiles.append(n_p // ROW_TILE)
    w_qkv = attn_w_qkv[0]
    w_q = (w_qkv[:, :d] * (HEAD_DIM ** -0.5 * LOG2_E)).astype(BF16)
    (q,) = _matmul(a2, w_q, BF16, name="attn_q")
    k, k_keep = _matmul(a2, w_qkv[:, d:2 * d].astype(BF16), BF16, keep_tiles, name="attn_k")
    v, v_keep = _matmul(a2, w_qkv[:, 2 * d:].astype(BF16), BF16, keep_tiles, name="attn_v")
    cache_rows = cache_k.shape[2]
    bias_p, bias_c, bias_n = _attn_biases(attn_rel_bias[0], cache_rows)
    o = _attn_prompt(q, k, v, bias_p)
    o = _attn_sample(q, k, v, cache_k[0].reshape(DEC_BATCH, cache_rows, d),
                     cache_v[0].reshape(DEC_BATCH, cache_rows, d), bias_c, bias_n, o)
    (m,) = _matmul(o, attn_w_o[0].astype(BF16), F32, name="attn_out")
    pp, ps = _stack([g1_1, sh2_1, sc2_1])
    x3, t3 = _ln1(x2, m, pp, ps, jnp.stack([ln_g[1, 0], ln_b[1, 0]]))
    pp, ps = _stack([g2_1])
    yp, ys = _moe(t3, x3, 1, router_w(1), expert_w, pp, ps, jnp.stack([ln_g[1, 1], ln_b[1, 1]]), False)

    n_heads = d // HEAD_DIM
    kp_rows = BATCH * keep_rows

    def prompt_cache(keep):
        return keep[:kp_rows].reshape(1, BATCH, keep_rows, n_heads, HEAD_DIM)

    def sample_cache(keep):
        return keep[kp_rows:kp_rows + _n_sample()].reshape(1, DEC_BATCH, DEC_SEQ, n_heads, HEAD_DIM)

    return (
        yp.reshape(BATCH, SEQ, d),
        ys.reshape(DEC_BATCH, DEC_SEQ, d),
        st_p[:, HALO - POOL_STATE:][None],
        prompt_cache(k_keep),
        prompt_cache(v_keep),
        h_s[:, HALO - POOL_STATE:][None],
        sample_cache(k_keep),
        sample_cache(v_keep),
    )
```

```python
import functools
import math

import jax
import jax.numpy as jnp
from jax import lax
from jax.experimental import pallas as pl
from jax.experimental.pallas import tpu as pltpu

D_MODEL = 4096
BATCH = 2
SEQ = 8192
DEPTH = 2
DEC_BATCH = 16
DEC_SEQ = 16
PAST_LEN = 4096
CHUNK = 64
POOL_WINDOWS = (2, 4, 8, 16)
POOL_STATE = 15
HEAD_DIM = 64
N_LEFT_CHUNKS = 8
REL_CLIP = 256
N_GROUPS = 4
EXPERTS_PER_GROUP = 8
D_EXPERT = 512
LN_EPS = 1e-5
NEG_INF = -1e30

ROW_TILE = 256
HALO = 16
EXPERT_TILE = 256
GMM_DOWN_COLS = 512
SCATTER_PRIORITY = 1
MM_TN = 1024
ATTN_Q = 256
ATTN_LANES = 1024
ATTN_GROUP_PAIRS = 8
ATTN_KW = ATTN_Q // 2 + N_LEFT_CHUNKS * CHUNK
ROUTER_LANES = 128
EXPERT_LANE0 = 32
VMEM_LIMIT = 56 * 1024 * 1024
LOG2_E = math.log2(math.e)

F32 = jnp.float32
BF16 = jnp.bfloat16


def _n_experts():
    return N_GROUPS * EXPERTS_PER_GROUP


def _n_prompt():
    return BATCH * SEQ


def _n_sample():
    return DEC_BATCH * DEC_SEQ


def _n_tot():
    return _n_prompt() + _n_sample()


def _max_expert_tiles():
    return (2 * _n_tot()) // EXPERT_TILE + _n_experts()


def _deepnorm_alpha():
    return (2 * DEPTH) ** 0.25


def _params(vmem=VMEM_LIMIT):
    return pltpu.CompilerParams(vmem_limit_bytes=vmem)


def _layer_norm(z, g, b):
    mu = jnp.mean(z, axis=-1, keepdims=True)
    zc = z - mu
    var = jnp.mean(zc * zc, axis=-1, keepdims=True)
    return zc * lax.rsqrt(var + LN_EPS) * g + b


def _ada_kernel(c_ref, w_ref, b_ref, o_ref):
    c = c_ref[...]
    s = (c / (1.0 + jnp.exp(-c))).astype(BF16)
    w = w_ref[0].astype(BF16)
    o_ref[0] = jnp.dot(s, w, preferred_element_type=F32) + b_ref[0]


def _ada_params(c_all, ada_w, ada_b):
    r, d = c_all.shape
    n = ada_w.shape[-1]
    tn = min(512, n)
    return pl.pallas_call(
        _ada_kernel,
        grid=(DEPTH, n // tn),
        in_specs=[
            pl.BlockSpec((r, d), lambda l, j: (0, 0)),
            pl.BlockSpec((1, d, tn), lambda l, j: (l, 0, j)),
            pl.BlockSpec((1, 1, tn), lambda l, j: (l, 0, j)),
        ],
        out_specs=pl.BlockSpec((1, r, tn), lambda l, j: (l, 0, j)),
        out_shape=jax.ShapeDtypeStruct((DEPTH, r, n), F32),
        compiler_params=_params(),
        name="ada_params",
    )(c_all, ada_w, ada_b.reshape(DEPTH, 1, n))


def _pool_kernel(x_ref, halo_ref, p_ref, w_ref, vec_ref, *rest, tm, tiles_per_seq, pos0, zero_first, aliased):
    if aliased:
        rest = rest[2:]
    x1_ref, t1_ref, st_ref, hbuf = rest
    i = pl.program_id(0)
    j = i % tiles_per_seq
    sh1 = p_ref[0, 0:1, :]
    sc1 = p_ref[0, 1:2, :]
    g1 = p_ref[0, 2:3, :]
    sh2 = p_ref[0, 3:4, :]
    sc2 = p_ref[0, 4:5, :]
    ls = vec_ref[0:1, :]
    ln_g = vec_ref[1:2, :]
    ln_b = vec_ref[2:3, :]

    x = x_ref[...]
    h = x * (1.0 + sc1) + sh1
    if zero_first:
        hh = halo_ref[...] * (1.0 + sc1) + sh1
        hh = jnp.where(j == 0, 0.0, hh)
    else:
        hh = halo_ref[...]
    hbuf[0:HALO, :] = hh
    hbuf[HALO:, :] = h

    pos = pos0 + j * tm + lax.broadcasted_iota(jnp.int32, (tm, 1), 0)
    gc = D_MODEL // len(POOL_WINDOWS)
    ys = []
    for g, w in enumerate(POOL_WINDOWS):
        s = hbuf[:, g * gc:(g + 1) * gc]
        k = 1
        while k < w:
            s = s + pltpu.roll(s, k, 0)
            k *= 2
        inv = 1.0 / jnp.minimum(pos + 1, w).astype(F32)
        pooled = s[HALO:, :] * inv - hbuf[HALO:, g * gc:(g + 1) * gc]
        ys.append(jnp.dot(pooled.astype(BF16), w_ref[g], preferred_element_type=F32))
    m = jnp.concatenate(ys, axis=-1) * ls
    x1 = _layer_norm(_deepnorm_alpha() * x + (1.0 + g1) * m, ln_g, ln_b)
    x1_ref[...] = x1
    t1_ref[...] = x1 * (1.0 + sc2) + sh2

    @pl.when(j == tiles_per_seq - 1)
    def _():
        st_ref[0] = hbuf[tm:, :]


def _pool_layer(xp, xs, hist_pad, pp, ps, pool_w, vecs):
    d = D_MODEL
    n_tot = _n_tot()
    gc = d // len(POOL_WINDOWS)
    np_tiles = _n_prompt() // ROW_TILE
    tps = SEQ // ROW_TILE
    hb = ROW_TILE // HALO
    common = [
        pl.BlockSpec((len(POOL_WINDOWS), gc, gc), lambda i: (0, 0, 0)),
        pl.BlockSpec((3, d), lambda i: (0, 0)),
    ]
    x1, t1, st_p = pl.pallas_call(
        functools.partial(_pool_kernel, tm=ROW_TILE, tiles_per_seq=tps, pos0=0, zero_first=True, aliased=False),
        grid=(np_tiles,),
        in_specs=[
            pl.BlockSpec((ROW_TILE, d), lambda i: (i, 0)),
            pl.BlockSpec((HALO, d), lambda i: (jnp.maximum(i * hb - 1, 0), 0)),
            pl.BlockSpec((1, 5, d), lambda i: (i // tps, 0, 0)),
        ] + common,
        out_specs=[
            pl.BlockSpec((ROW_TILE, d), lambda i: (i, 0)),
            pl.BlockSpec((ROW_TILE, d), lambda i: (i, 0)),
            pl.BlockSpec((1, HALO, d), lambda i: (i // tps, 0, 0)),
        ],
        out_shape=[
            jax.ShapeDtypeStruct((n_tot, d), F32),
            jax.ShapeDtypeStruct((n_tot, d), F32),
            jax.ShapeDtypeStruct((BATCH, HALO, d), F32),
        ],
        scratch_shapes=[pltpu.VMEM((ROW_TILE + HALO, d), F32)],
        compiler_params=_params(),
        name="pool_prompt",
    )(xp, xp, pp, pool_w, vecs)

    off = _n_prompt() // DEC_SEQ
    x1, t1, h_s = pl.pallas_call(
        functools.partial(_pool_kernel, tm=DEC_SEQ, tiles_per_seq=1, pos0=PAST_LEN, zero_first=False, aliased=True),
        grid=(DEC_BATCH,),
        in_specs=[
            pl.BlockSpec((DEC_SEQ, d), lambda i: (i, 0)),
            pl.BlockSpec((HALO, d), lambda i: (i, 0)),
            pl.BlockSpec((1, 5, d), lambda i: (i, 0, 0)),
        ] + common + [
            pl.BlockSpec(memory_space=pl.ANY),
            pl.BlockSpec(memory_space=pl.ANY),
        ],
        out_specs=[
            pl.BlockSpec((DEC_SEQ, d), lambda i: (off + i, 0)),
            pl.BlockSpec((DEC_SEQ, d), lambda i: (off + i, 0)),
            pl.BlockSpec((1, HALO, d), lambda i: (i, 0, 0)),
        ],
        out_shape=[
            jax.ShapeDtypeStruct((n_tot, d), F32),
            jax.ShapeDtypeStruct((n_tot, d), F32),
            jax.ShapeDtypeStruct((DEC_BATCH, HALO, d), F32),
        ],
        scratch_shapes=[pltpu.VMEM((DEC_SEQ + HALO, d), F32)],
        input_output_aliases={5: 0, 6: 1},
        compiler_params=_params(),
        name="pool_sample",
    )(xs, hist_pad, ps, pool_w, vecs, x1, t1)
    return x1, t1, st_p, h_s


def _router_kernel(t_ref, wr_ref, br_ref, carry_ref, *rest, tm, aliased):
    if aliased:
        rest = rest[2:]
    ri_ref, rw_ref, cnt_out_ref, cnt = rest
    i = pl.program_id(0)

    @pl.when(i == 0)
    def _():
        cnt[...] = carry_ref[...]

    t = t_ref[...]
    t_hi = t.astype(BF16)
    t_lo = (t - t_hi.astype(F32)).astype(BF16)
    logits = (jnp.dot(t_hi, wr_ref[0], preferred_element_type=F32)
              + jnp.dot(t_lo, wr_ref[0], preferred_element_type=F32)
              + jnp.dot(t_hi, wr_ref[1], preferred_element_type=F32)) + br_ref[...]
    lane = lax.broadcasted_iota(jnp.int32, (tm, ROUTER_LANES), 1).astype(F32)
    big = float(ROUTER_LANES)
    ninf = -jnp.inf

    lg = jnp.where(lane < N_GROUPS, logits, ninf)
    mg = jnp.max(lg, axis=-1, keepdims=True)
    gidx = jnp.min(jnp.where(lg == mg, lane, big), axis=-1, keepdims=True)
    p_group = 1.0 / jnp.sum(jnp.exp(lg - mg), axis=-1, keepdims=True)

    lo = EXPERT_LANE0 + gidx * EXPERTS_PER_GROUP
    le = jnp.where((lane >= lo) & (lane < lo + EXPERTS_PER_GROUP), logits, ninf)
    m1 = jnp.max(le, axis=-1, keepdims=True)
    i1 = jnp.min(jnp.where(le == m1, lane, big), axis=-1, keepdims=True)
    le2 = jnp.where(lane == i1, ninf, le)
    m2 = jnp.max(le2, axis=-1, keepdims=True)
    i2 = jnp.min(jnp.where(le2 == m2, lane, big), axis=-1, keepdims=True)
    e2 = jnp.exp(m2 - m1)
    w1 = p_group / (1.0 + e2)
    w2 = p_group * e2 / (1.0 + e2)

    hit1 = lane == i1
    hit2 = lane == i2
    onehot = jnp.where(hit1 | hit2, 1.0, 0.0)
    row = lax.broadcasted_iota(jnp.int32, (tm, tm), 0)
    col = lax.broadcasted_iota(jnp.int32, (tm, tm), 1)
    tri = jnp.where(col < row, 1.0, 0.0).astype(BF16)
    before = jnp.dot(tri, onehot.astype(BF16), preferred_element_type=F32) + cnt[...]
    r1 = jnp.sum(jnp.where(hit1, before, 0.0), axis=-1, keepdims=True)
    r2 = jnp.sum(jnp.where(hit2, before, 0.0), axis=-1, keepdims=True)
    cnt[...] = cnt[...] + jnp.sum(onehot, axis=0, keepdims=True)

    zero = jnp.zeros_like(logits)
    ints = jnp.where(lane == 0, i1 - EXPERT_LANE0,
                     jnp.where(lane == 1, i2 - EXPERT_LANE0,
                               jnp.where(lane == 2, r1, jnp.where(lane == 3, r2, zero))))
    ri_ref[...] = ints.astype(jnp.int32)
    rw_ref[...] = jnp.where(lane == 0, w1, jnp.where(lane == 1, w2, zero))
    cnt_out_ref[...] = cnt[...]


def _router(t, wr, br):
    d = D_MODEL
    n_tot = _n_tot()
    outs_shape = [
        jax.ShapeDtypeStruct((n_tot, ROUTER_LANES), jnp.int32),
        jax.ShapeDtypeStruct((n_tot, ROUTER_LANES), F32),
        jax.ShapeDtypeStruct((1, ROUTER_LANES), F32),
    ]
    fixed = [
        pl.BlockSpec((2, d, ROUTER_LANES), lambda i: (0, 0, 0)),
        pl.BlockSpec((1, ROUTER_LANES), lambda i: (0, 0)),
        pl.BlockSpec((1, ROUTER_LANES), lambda i: (0, 0)),
    ]
    ri, rw, cnt = pl.pallas_call(
        functools.partial(_router_kernel, tm=ROW_TILE, aliased=False),
        grid=(_n_prompt() // ROW_TILE,),
        in_specs=[pl.BlockSpec((ROW_TILE, d), lambda i: (i, 0))] + fixed,
        out_specs=[
            pl.BlockSpec((ROW_TILE, ROUTER_LANES), lambda i: (i, 0)),
            pl.BlockSpec((ROW_TILE, ROUTER_LANES), lambda i: (i, 0)),
            pl.BlockSpec((1, ROUTER_LANES), lambda i: (0, 0)),
        ],
        out_shape=outs_shape,
        scratch_shapes=[pltpu.VMEM((1, ROUTER_LANES), F32)],
        compiler_params=_params(),
        name="router_prompt",
    )(t, wr, br, jnp.zeros((1, ROUTER_LANES), F32))
    off = _n_prompt() // DEC_SEQ
    ri, rw, cnt = pl.pallas_call(
        functools.partial(_router_kernel, tm=DEC_SEQ, aliased=True),
        grid=(DEC_BATCH,),
        in_specs=[pl.BlockSpec((DEC_SEQ, d), lambda i: (off + i, 0))] + fixed + [
            pl.BlockSpec(memory_space=pl.ANY),
            pl.BlockSpec(memory_space=pl.ANY),
        ],
        out_specs=[
            pl.BlockSpec((DEC_SEQ, ROUTER_LANES), lambda i: (off + i, 0)),
            pl.BlockSpec((DEC_SEQ, ROUTER_LANES), lambda i: (off + i, 0)),
            pl.BlockSpec((1, ROUTER_LANES), lambda i: (0, 0)),
        ],
        out_shape=outs_shape,
        scratch_shapes=[pltpu.VMEM((1, ROUTER_LANES), F32)],
        input_output_aliases={4: 0, 5: 1},
        compiler_params=_params(),
        name="router_sample",
    )(t, wr, br, cnt, ri, rw)
    return ri, rw, cnt


def _n_slots():
    return (_max_expert_tiles() + 2) * EXPERT_TILE


def _slot_kernel(pos_ref, code_ref):
    n_tot = _n_tot()
    n_pairs = 2 * n_tot

    def init(s, c):
        code_ref[s] = n_pairs + s
        return c

    def fill(r, c):
        code_ref[pos_ref[2 * r]] = r
        code_ref[pos_ref[2 * r + 1]] = n_tot + r
        return c

    lax.fori_loop(0, _n_slots(), init, 0, unroll=8)
    lax.fori_loop(0, n_tot, fill, 0, unroll=8)


def _slot_codes(pos_flat):
    return pl.pallas_call(
        _slot_kernel,
        in_specs=[pl.BlockSpec(memory_space=pltpu.SMEM)],
        out_specs=pl.BlockSpec(memory_space=pltpu.SMEM),
        out_shape=jax.ShapeDtypeStruct((_n_slots(),), jnp.int32),
        name="moe_slots",
    )(pos_flat)


def _gmm_kernel(te_ref, nu_ref, src_ref, code_ref, t_hbm, wg_ref, wu_ref, wd_ref, y_hbm,
                xbuf0, xbuf1, ybuf0, ybuf1, gsem, ssem):
    del te_ref
    i = pl.program_id(0)
    nu = nu_ref[0]
    tg = EXPERT_TILE
    warm0 = _max_expert_tiles()
    xbufs = (xbuf0, xbuf1)
    ybufs = (ybuf0, ybuf1)

    def gather(tile, k, r):
        src = src_ref[tile * tg + r]
        return pltpu.make_async_copy(t_hbm.at[pl.ds(src, 1)], xbufs[k].at[pl.ds(r, 1)], gsem.at[k])

    def scatter(tile, k, r):
        dst = code_ref[tile * tg + r]
        return pltpu.make_async_copy(ybufs[k].at[pl.ds(r, 1)], y_hbm.at[pl.ds(dst, 1)], ssem.at[k])

    def wait_gather(k):
        pltpu.make_async_copy(t_hbm.at[pl.ds(0, tg)], xbufs[k], gsem.at[k]).wait()

    def wait_scatter(k):
        pltpu.make_async_copy(ybufs[k], y_hbm.at[pl.ds(0, tg)], ssem.at[k]).wait()

    @pl.when(i == 0)
    def _():
        ybuf0[...] = jnp.zeros_like(ybuf0)
        ybuf1[...] = jnp.zeros_like(ybuf1)
        for r in range(tg):
            gather(0, 0, r).start()
        for r in range(tg):
            scatter(warm0, 0, r).start(priority=SCATTER_PRIORITY)

    def step(k):
        other = 1 - k
        nxt = jnp.minimum(i + 1, nu - 1)
        prev = jnp.where(i == 0, warm0 + 1, i - 1)
        for r in range(tg):
            gather(nxt, other, r).start()
        for r in range(tg):
            scatter(prev, other, r).start(priority=SCATTER_PRIORITY)
        wait_gather(k)
        wait_scatter(k)
        x = xbufs[k][...].astype(BF16)
        a = jnp.dot(x, wg_ref[0, 0], preferred_element_type=F32)
        b = jnp.dot(x, wu_ref[0, 0], preferred_element_type=F32)
        hid = ((a / (1.0 + jnp.exp(-a))) * b).astype(BF16)
        for c in range(D_MODEL // GMM_DOWN_COLS):
            cols = slice(c * GMM_DOWN_COLS, (c + 1) * GMM_DOWN_COLS)
            ybufs[k][:, cols] = jnp.dot(hid, wd_ref[0, 0, :, cols], preferred_element_type=F32)

        @pl.when(i == nu - 1)
        def _():
            for r in range(tg):
                scatter(i, k, r).start(priority=SCATTER_PRIORITY)
            wait_scatter(other)
            wait_scatter(k)
            wait_gather(other)

    for k in range(2):
        @pl.when((i < nu) & (i % 2 == k))
        def _(k=k):
            step(k)


def _gmm(t, code, tile_expert, n_used, layer, wg, wu, wd):
    d = D_MODEL
    tiles = _max_expert_tiles()
    n_tot = _n_tot()
    src = jnp.minimum(jnp.where(code >= n_tot, code - n_tot, code), n_tot - 1)

    def wsel(i, te, nu, src, code):
        return (layer, te[jnp.minimum(i, nu[0] - 1)], 0, 0)

    return pl.pallas_call(
        _gmm_kernel,
        grid_spec=pltpu.PrefetchScalarGridSpec(
            num_scalar_prefetch=4,
            grid=(tiles,),
            in_specs=[
                pl.BlockSpec(memory_space=pl.ANY),
                pl.BlockSpec((1, 1, d, D_EXPERT), wsel),
                pl.BlockSpec((1, 1, d, D_EXPERT), wsel),
                pl.BlockSpec((1, 1, D_EXPERT, d), wsel),
            ],
            out_specs=pl.BlockSpec(memory_space=pl.ANY),
            scratch_shapes=[
                pltpu.VMEM((EXPERT_TILE, d), F32),
                pltpu.VMEM((EXPERT_TILE, d), F32),
                pltpu.VMEM((EXPERT_TILE, d), F32),
                pltpu.VMEM((EXPERT_TILE, d), F32),
                pltpu.SemaphoreType.DMA((2,)),
                pltpu.SemaphoreType.DMA((2,)),
            ],
        ),
        out_shape=jax.ShapeDtypeStruct((2 * _n_tot() + _n_slots(), d), F32),
        compiler_params=_params(),
        name="moe_experts",
    )(tile_expert, n_used, src, code, t, wg, wu, wd)


def _post_kernel(x_ref, y0_ref, y1_ref, rw_ref, p_ref, vec_ref, *rest, emit_next, aliased):
    if aliased:
        rest = rest[(2 if emit_next else 1):]
    g2 = p_ref[0, 0:1, :]
    f = rw_ref[:, 0:1] * y0_ref[...] + rw_ref[:, 1:2] * y1_ref[...]
    xo = _layer_norm(_deepnorm_alpha() * x_ref[...] + (1.0 + g2) * f, vec_ref[0:1, :], vec_ref[1:2, :])
    rest[0][...] = xo
    if emit_next:
        rest[1][...] = (xo * (1.0 + p_ref[0, 2:3, :]) + p_ref[0, 1:2, :]).astype(BF16)


def _post(x, y, rw, pp, ps, vecs, emit_next):
    d = D_MODEL
    n_p, n_tot = _n_prompt(), _n_tot()
    np_tiles = n_p // ROW_TILE
    tps = SEQ // ROW_TILE
    off = n_p // DEC_SEQ
    npar = pp.shape[1]

    def call(tm, grid, roff, par, bmap, prev, name):
        aliased = prev is not None
        y1_off = n_tot // tm
        in_specs = [
            pl.BlockSpec((tm, d), lambda i: (roff + i, 0)),
            pl.BlockSpec((tm, d), lambda i: (roff + i, 0)),
            pl.BlockSpec((tm, d), lambda i: (y1_off + roff + i, 0)),
            pl.BlockSpec((tm, ROUTER_LANES), lambda i: (roff + i, 0)),
            pl.BlockSpec((1, npar, d), lambda i: (bmap(i), 0, 0)),
            pl.BlockSpec((2, d), lambda i: (0, 0)),
        ]
        args = [x, y, y, rw, par, vecs]
        if emit_next:
            oroff, rows = roff, n_tot
        else:
            oroff, rows = 0, grid * tm
        out_specs = [pl.BlockSpec((tm, d), lambda i: (oroff + i, 0))]
        out_shape = [jax.ShapeDtypeStruct((rows, d), F32)]
        if emit_next:
            out_specs.append(pl.BlockSpec((tm, d), lambda i: (oroff + i, 0)))
            out_shape.append(jax.ShapeDtypeStruct((rows, d), BF16))
        aliases = {}
        if aliased:
            for k, a in enumerate(prev):
                in_specs.append(pl.BlockSpec(memory_space=pl.ANY))
                args.append(a)
                aliases[6 + k] = k
        return pl.pallas_call(
            functools.partial(_post_kernel, emit_next=emit_next, aliased=aliased),
            grid=(grid,),
            in_specs=in_specs,
            out_specs=out_specs,
            out_shape=out_shape,
            input_output_aliases=aliases,
            compiler_params=_params(),
            name=name,
        )(*args)

    outs_p = call(ROW_TILE, np_tiles, 0, pp, lambda i: i // tps, None, "moe_post_prompt")
    if emit_next:
        return call(DEC_SEQ, DEC_BATCH, off, ps, lambda i: i, outs_p, "moe_post_sample")
    outs_s = call(DEC_SEQ, DEC_BATCH, off, ps, lambda i: i, None, "moe_post_sample")
    return outs_p[0], outs_s[0]


def _moe(t, x, layer, router_w, expert_w, pp, ps, vecs, emit_next):
    wr, br = router_w
    ri, rw, cnt = _router(t, wr, br)
    ne = _n_experts()
    counts = cnt[0, EXPERT_LANE0:EXPERT_LANE0 + ne].astype(jnp.int32)
    tiles_e = (counts + EXPERT_TILE - 1) // EXPERT_TILE
    tile_end = jnp.cumsum(tiles_e)
    pad_off = (tile_end - tiles_e) * EXPERT_TILE
    n_used = tile_end[-1:].astype(jnp.int32)
    tile_ids = jnp.arange(_max_expert_tiles(), dtype=jnp.int32)
    tile_expert = jnp.minimum(jnp.sum(tile_ids[:, None] >= tile_end[None, :], axis=1), ne - 1).astype(jnp.int32)
    eid = ri[:, 0:2]
    rank = ri[:, 2:4]
    onehot = eid[:, :, None] == jnp.arange(ne, dtype=jnp.int32)[None, None, :]
    pos = jnp.sum(jnp.where(onehot, pad_off[None, None, :], 0), axis=-1) + rank
    code = _slot_codes(pos.astype(jnp.int32).reshape(-1))
    y = _gmm(t, code, tile_expert, n_used, layer, *expert_w)
    return _post(x, y, rw, pp, ps, vecs, emit_next)


def _mm_kernel(a_ref, w_ref, *out_refs, keep_tiles):
    acc = jnp.dot(a_ref[...], w_ref[...], preferred_element_type=F32)
    out_refs[0][...] = acc.astype(out_refs[0].dtype)
    if keep_tiles:
        i = pl.program_id(1)
        keep = functools.reduce(jnp.logical_or, [i == k for k in keep_tiles])

        @pl.when(keep)
        def _():
            out_refs[1][...] = acc


def _matmul(a, w, out_dtype, keep_tiles=(), name="matmul"):
    m, k = a.shape
    n_out = w.shape[1]
    tm = ROW_TILE
    tn = min(MM_TN, n_out)
    out_specs = [pl.BlockSpec((tm, tn), lambda j, i: (i, j))]
    out_shape = [jax.ShapeDtypeStruct((m, n_out), out_dtype)]
    if keep_tiles:
        def slot(j, i):
            return (sum((i > b).astype(jnp.int32) for b in keep_tiles), j)

        out_specs.append(pl.BlockSpec((tm, tn), slot))
        out_shape.append(jax.ShapeDtypeStruct((len(keep_tiles) * tm, n_out), F32))
    outs = pl.pallas_call(
        functools.partial(_mm_kernel, keep_tiles=tuple(keep_tiles)),
        grid=(n_out // tn, m // tm),
        in_specs=[
            pl.BlockSpec((tm, k), lambda j, i: (i, 0)),
            pl.BlockSpec((k, tn), lambda j, i: (0, j)),
        ],
        out_specs=out_specs,
        out_shape=out_shape,
        compiler_params=_params(),
        name=name,
    )(a, w)
    return outs


def _lane_half(shape, hh):
    return (lax.broadcasted_iota(jnp.int32, shape, 1) < HEAD_DIM) == (hh == 0)


def _attend_group(problems):
    scores, tops = [], []
    for q2, hh, pieces in problems:
        qm = jnp.where(_lane_half(q2.shape, hh), q2, jnp.zeros_like(q2))
        ss = [lax.dot_general(qm, k, (((1,), (1,)), ((), ())), preferred_element_type=F32) + bias
              for k, _, bias, _ in pieces]
        m = None
        for s, (_, _, _, pen) in zip(ss, pieces):
            rm = jnp.max(s, axis=-1, keepdims=True) + pen
            m = rm if m is None else jnp.maximum(m, rm)
        scores.append(ss)
        tops.append(m)
    outs = []
    for (q2, hh, pieces), ss, m in zip(problems, scores, tops):
        o = None
        for s, (_, v, _, pen) in zip(ss, pieces):
            vm = jnp.where(_lane_half(v.shape, hh), v, jnp.ones_like(v))
            po = jnp.dot(jnp.exp2(s - (m - pen)).astype(BF16), vm, preferred_element_type=F32)
            o = po if o is None else o + po
        outs.append(o)
    return outs


def _merge_heads(o_a, o_b):
    first = _lane_half(o_a.shape, 0)
    o = jnp.where(first, o_a, o_b)
    den = jnp.where(first, pltpu.roll(o_a, HEAD_DIM, 1), pltpu.roll(o_b, HEAD_DIM, 1))
    return o / den


def _attn_prompt_kernel(q_ref, k0_ref, k1_ref, k2_ref, v0_ref, v1_ref, v2_ref, bias_ref, o_ref):
    cg = pl.program_id(2)
    k_refs = (k0_ref, k1_ref, k2_ref)
    v_refs = (v0_ref, v1_ref, v2_ref)
    pens = (jnp.where(cg >= 2, 0.0, NEG_INF), jnp.where(cg >= 1, 0.0, NEG_INF), 0.0)
    qh = ATTN_Q // 2
    n_pairs = ATTN_LANES // (2 * HEAD_DIM)

    def cols(p):
        return slice(p * 2 * HEAD_DIM, (p + 1) * 2 * HEAD_DIM)

    def pieces(p, hh, half):
        out = []
        lo, hi = half * qh, half * qh + ATTN_KW
        for kb in range(3):
            a, b = max(lo, kb * ATTN_Q), min(hi, (kb + 1) * ATTN_Q)
            if a < b:
                rows = slice(a - kb * ATTN_Q, b - kb * ATTN_Q)
                out.append((k_refs[kb][rows, cols(p)], v_refs[kb][rows, cols(p)],
                            bias_ref[2 * p + hh, :, a - lo:b - lo], pens[kb]))
        return out

    for p0 in range(0, n_pairs, ATTN_GROUP_PAIRS):
        keys = [(p, hh, half) for p in range(p0, p0 + ATTN_GROUP_PAIRS) for hh in range(2) for half in range(2)]
        outs = _attend_group([(q_ref[half * qh:(half + 1) * qh, cols(p)], hh, pieces(p, hh, half))
                              for p, hh, half in keys])
        res = dict(zip(keys, outs))
        for p in range(p0, p0 + ATTN_GROUP_PAIRS):
            for half in range(2):
                o_ref[half * qh:(half + 1) * qh, cols(p)] = _merge_heads(
                    res[p, 0, half], res[p, 1, half]).astype(o_ref.dtype)


def _attn_prompt(q, k, v, bias):
    d = D_MODEL
    cgs = SEQ // ATTN_Q
    hpg = ATTN_LANES // HEAD_DIM

    def kmap(back):
        return lambda hg, b, c: (b * cgs + jnp.maximum(c - back, 0), hg)

    blk = (ATTN_Q, ATTN_LANES)
    return pl.pallas_call(
        _attn_prompt_kernel,
        grid=(d // ATTN_LANES, BATCH, cgs),
        in_specs=[
            pl.BlockSpec(blk, lambda hg, b, c: (b * cgs + c, hg)),
            pl.BlockSpec(blk, kmap(2)), pl.BlockSpec(blk, kmap(1)), pl.BlockSpec(blk, kmap(0)),
            pl.BlockSpec(blk, kmap(2)), pl.BlockSpec(blk, kmap(1)), pl.BlockSpec(blk, kmap(0)),
            pl.BlockSpec((hpg, ATTN_Q // 2, ATTN_KW), lambda hg, b, c: (hg, 0, 0)),
        ],
        out_specs=pl.BlockSpec(blk, lambda hg, b, c: (b * cgs + c, hg)),
        out_shape=jax.ShapeDtypeStruct((_n_tot(), d), BF16),
        compiler_params=_params(),
        name="attn_prompt",
    )(q, k, k, k, v, v, v, bias)


def _attn_sample_kernel(q_ref, kn_ref, vn_ref, ck_ref, cv_ref, bc_ref, bn_ref, o_in_ref, o_ref):
    del o_in_ref
    n_pairs = ATTN_LANES // (2 * HEAD_DIM)
    problems = []
    for p in range(n_pairs):
        cols = slice(p * 2 * HEAD_DIM, (p + 1) * 2 * HEAD_DIM)
        kc = ck_ref[0, :, cols].astype(BF16)
        vc = cv_ref[0, :, cols].astype(BF16)
        for hh in range(2):
            problems.append((q_ref[:, cols], hh, [(kc, vc, bc_ref[2 * p + hh], 0.0),
                                                  (kn_ref[:, cols], vn_ref[:, cols], bn_ref[2 * p + hh], 0.0)]))
    outs = _attend_group(problems)
    for p in range(n_pairs):
        cols = slice(p * 2 * HEAD_DIM, (p + 1) * 2 * HEAD_DIM)
        o_ref[:, cols] = _merge_heads(outs[2 * p], outs[2 * p + 1]).astype(o_ref.dtype)


def _attn_sample(q, k, v, ck, cv, bias_c, bias_n, o):
    d = D_MODEL
    off = _n_prompt() // DEC_SEQ
    hpg = ATTN_LANES // HEAD_DIM
    rows = ck.shape[1]
    new = pl.BlockSpec((DEC_SEQ, ATTN_LANES), lambda hg, b: (off + b, hg))
    cache = pl.BlockSpec((1, rows, ATTN_LANES), lambda hg, b: (b, 0, hg))
    return pl.pallas_call(
        _attn_sample_kernel,
        grid=(d // ATTN_LANES, DEC_BATCH),
        in_specs=[
            new, new, new, cache, cache,
            pl.BlockSpec((hpg, DEC_SEQ, rows), lambda hg, b: (hg, 0, 0)),
            pl.BlockSpec((hpg, DEC_SEQ, DEC_SEQ), lambda hg, b: (hg, 0, 0)),
            pl.BlockSpec(memory_space=pl.ANY),
        ],
        out_specs=new,
        out_shape=jax.ShapeDtypeStruct((_n_tot(), d), BF16),
        input_output_aliases={7: 0},
        compiler_params=_params(),
        name="attn_sample",
    )(q, k, v, ck, cv, bias_c, bias_n, o)


def _rel_index(dist):
    return jnp.clip(dist, -REL_CLIP, REL_CLIP) + REL_CLIP


def _attn_biases(rel_bias, cache_rows):
    tbl = rel_bias * LOG2_E
    n_heads = tbl.shape[0]
    nq = ATTN_Q // 2
    kw = ATTN_KW
    back = N_LEFT_CHUNKS * CHUNK
    span = kw + nq - 1
    u = tbl[:, _rel_index(back + (nq - 1) - jnp.arange(span))]
    rep = jnp.broadcast_to(jnp.pad(u, ((0, 0), (0, 1)))[:, None, :], (n_heads, nq, span + 1))
    toep = rep.reshape(n_heads, nq * (span + 1))[:, :nq * span].reshape(n_heads, nq, span)
    qi = jnp.arange(nq)[:, None]
    kj = jnp.arange(kw)[None, :]
    qc = qi // CHUNK
    kc = kj // CHUNK - back // CHUNK
    vis = (kc <= qc) & (kc >= qc - N_LEFT_CHUNKS)
    bias_p = jnp.where(vis[None], toep[:, :, nq - 1:], NEG_INF)
    si = jnp.arange(DEC_SEQ)[:, None]
    dist_c = si + cache_rows - jnp.arange(cache_rows)[None, :]
    dist_n = si - jnp.arange(DEC_SEQ)[None, :]
    return bias_p, tbl[:, _rel_index(dist_c)], tbl[:, _rel_index(dist_n)]


def _ln1_kernel(x_ref, m_ref, p_ref, vec_ref, *rest, aliased):
    if aliased:
        rest = rest[2:]
    x1_ref, t1_ref = rest
    g1 = p_ref[0, 0:1, :]
    x1 = _layer_norm(_deepnorm_alpha() * x_ref[...] + (1.0 + g1) * m_ref[...], vec_ref[0:1, :], vec_ref[1:2, :])
    x1_ref[...] = x1
    t1_ref[...] = x1 * (1.0 + p_ref[0, 2:3, :]) + p_ref[0, 1:2, :]


def _ln1(x, m, pp, ps, vecs):
    d = D_MODEL
    n_tot = _n_tot()
    tps = SEQ // ROW_TILE
    off = _n_prompt() // DEC_SEQ
    out_shape = [jax.ShapeDtypeStruct((n_tot, d), F32), jax.ShapeDtypeStruct((n_tot, d), F32)]

    def call(tm, grid, roff, par, bmap, prev, name):
        blk = pl.BlockSpec((tm, d), lambda i: (roff + i, 0))
        in_specs = [blk, blk, pl.BlockSpec((1, 3, d), lambda i: (bmap(i), 0, 0)), pl.BlockSpec((2, d), lambda i: (0, 0))]
        args = [x, m, par, vecs]
        aliases = {}
        if prev is not None:
            in_specs += [pl.BlockSpec(memory_space=pl.ANY)] * 2
            args += list(prev)
            aliases = {4: 0, 5: 1}
        return pl.pallas_call(
            functools.partial(_ln1_kernel, aliased=prev is not None),
            grid=(grid,),
            in_specs=in_specs,
            out_specs=[blk, blk],
            out_shape=out_shape,
            input_output_aliases=aliases,
            compiler_params=_params(),
            name=name,
        )(*args)

    outs = call(ROW_TILE, _n_prompt() // ROW_TILE, 0, pp, lambda i: i // tps, None, "attn_post_prompt")
    return call(DEC_SEQ, DEC_BATCH, off, ps, lambda i: i, outs, "attn_post_sample")


def _split_mods(mods_l):
    return jnp.split(mods_l, 6, axis=-1)


def _stack(vectors):
    st = jnp.stack(vectors, axis=1)
    return st[:BATCH], st[BATCH:BATCH + DEC_BATCH]


def _router_weights(rg_w, rg_b, re_w, re_b):
    d = D_MODEL
    ne = _n_experts()
    wr = jnp.zeros((d, ROUTER_LANES), F32)
    wr = wr.at[:, :N_GROUPS].set(rg_w)
    wr = wr.at[:, EXPERT_LANE0:EXPERT_LANE0 + ne].set(jnp.transpose(re_w, (1, 0, 2)).reshape(d, ne))
    br = jnp.zeros((1, ROUTER_LANES), F32)
    br = br.at[0, :N_GROUPS].set(rg_b)
    br = br.at[0, EXPERT_LANE0:EXPERT_LANE0 + ne].set(re_b.reshape(ne))
    wr_hi = wr.astype(BF16)
    wr_lo = (wr - wr_hi.astype(F32)).astype(BF16)
    return jnp.stack([wr_hi, wr_lo]), br


def kernel(x_prompt, x_sample, c_prompt, c_sample, state_pool, cache_k, cache_v, ada_w, ada_b, ln_g, ln_b,
           pool_w, pool_scale, attn_w_qkv, attn_w_o, attn_rel_bias, router_group_w, router_group_b,
           router_expert_w, router_expert_b, moe_w_gate, moe_w_up, moe_w_down):
    d = D_MODEL
    n_p = _n_prompt()
    xp = x_prompt.reshape(n_p, d)
    xs = x_sample.reshape(_n_sample(), d)

    n_c = BATCH + DEC_BATCH
    c_rows = 32
    c_all = jnp.concatenate([c_prompt, c_sample, jnp.zeros((c_rows - n_c, d), F32)], axis=0)
    mods = _ada_params(c_all, ada_w, ada_b)
    sh1_0, sc1_0, g1_0, sh2_0, sc2_0, g2_0 = _split_mods(mods[0])
    sh1_1, sc1_1, g1_1, sh2_1, sc2_1, g2_1 = _split_mods(mods[1])

    expert_w = (moe_w_gate.astype(BF16), moe_w_up.astype(BF16), moe_w_down.astype(BF16))

    def router_w(l):
        return _router_weights(router_group_w[l], router_group_b[l], router_expert_w[l], router_expert_b[l])

    hist_pad = jnp.pad(state_pool[0], ((0, 0), (HALO - POOL_STATE, 0), (0, 0))).reshape(DEC_BATCH * HALO, d)
    pp, ps = _stack([sh1_0, sc1_0, g1_0, sh2_0, sc2_0])
    vecs = jnp.stack([pool_scale[0], ln_g[0, 0], ln_b[0, 0]])
    x1, t1, st_p, h_s = _pool_layer(xp, xs, hist_pad, pp, ps, pool_w[0].astype(BF16), vecs)
    pp, ps = _stack([g2_0, sh1_1, sc1_1])
    x2, a2 = _moe(t1, x1, 0, router_w(0), expert_w, pp, ps, jnp.stack([ln_g[0, 1], ln_b[0, 1]]), True)

    tiles_per_seq = SEQ // ROW_TILE
    keep_rows = min(N_LEFT_CHUNKS * CHUNK, SEQ)
    kt = keep_rows // ROW_TILE
    keep_tiles = [b * tiles_per_seq + tiles_per_seq - kt + r for b in range(BATCH) for r in range(kt)]
    keep_tiles.append(n_p // ROW_TILE)
    w_qkv = attn_w_qkv[0]
    w_q = (w_qkv[:, :d] * (HEAD_DIM ** -0.5 * LOG2_E)).astype(BF16)
    (q,) = _matmul(a2, w_q, BF16, name="attn_q")
    k, k_keep = _matmul(a2, w_qkv[:, d:2 * d].astype(BF16), BF16, keep_tiles, name="attn_k")
    v, v_keep = _matmul(a2, w_qkv[:, 2 * d:].astype(BF16), BF16, keep_tiles, name="attn_v")
    cache_rows = cache_k.shape[2]
    bias_p, bias_c, bias_n = _attn_biases(attn_rel_bias[0], cache_rows)
    o = _attn_prompt(q, k, v, bias_p)
    o = _attn_sample(q, k, v, cache_k[0].reshape(DEC_BATCH, cache_rows, d),
                     cache_v[0].reshape(DEC_BATCH, cache_rows, d), bias_c, bias_n, o)
    (m,) = _matmul(o, attn_w_o[0].astype(BF16), F32, name="attn_out")
    pp, ps = _stack([g1_1, sh2_1, sc2_1])
    x3, t3 = _ln1(x2, m, pp, ps, jnp.stack([ln_g[1, 0], ln_b[1, 0]]))
    pp, ps = _stack([g2_1])
    yp, ys = _moe(t3, x3, 1, router_w(1), expert_w, pp, ps, jnp.stack([ln_g[1, 1], ln_b[1, 1]]), False)

    n_heads = d // HEAD_DIM
    kp_rows = BATCH * keep_rows

    def prompt_cache(keep):
        return keep[:kp_rows].reshape(1, BATCH, keep_rows, n_heads, HEAD_DIM)

    def sample_cache(keep):
        return keep[kp_rows:kp_rows + _n_sample()].reshape(1, DEC_BATCH, DEC_SEQ, n_heads, HEAD_DIM)

    return (
        yp.reshape(BATCH, SEQ, d),
        ys.reshape(DEC_BATCH, DEC_SEQ, d),
        st_p[:, HALO - POOL_STATE:][None],
        prompt_cache(k_keep),
        prompt_cache(v_keep),
        h_s[:, HALO - POOL_STATE:][None],
        sample_cache(k_keep),
        sample_cache(v_keep),
    )
```

```python
import functools
import math

import jax
import jax.numpy as jnp
from jax import lax
from jax.experimental import pallas as pl
from jax.experimental.pallas import tpu as pltpu

D_MODEL = 4096
BATCH = 2
SEQ = 8192
DEPTH = 2
DEC_BATCH = 16
DEC_SEQ = 16
PAST_LEN = 4096
CHUNK = 64
POOL_WINDOWS = (2, 4, 8, 16)
POOL_STATE = 15
HEAD_DIM = 64
N_LEFT_CHUNKS = 8
REL_CLIP = 256
N_GROUPS = 4
EXPERTS_PER_GROUP = 8
D_EXPERT = 512
LN_EPS = 1e-5
NEG_INF = -1e30

ROW_TILE = 256
HALO = 16
EXPERT_TILE = 256
GMM_DOWN_COLS = 512
SCATTER_PRIORITY = 1
MM_TN = 2048
ATTN_Q = 256
ATTN_LANES = 1024
ATTN_GROUP_PAIRS = 8
ATTN_KW = ATTN_Q // 2 + N_LEFT_CHUNKS * CHUNK
ROUTER_LANES = 128
EXPERT_LANE0 = 32
VMEM_LIMIT = 56 * 1024 * 1024
LOG2_E = math.log2(math.e)

F32 = jnp.float32
BF16 = jnp.bfloat16


def _n_experts():
    return N_GROUPS * EXPERTS_PER_GROUP


def _n_prompt():
    return BATCH * SEQ


def _n_sample():
    return DEC_BATCH * DEC_SEQ


def _n_tot():
    return _n_prompt() + _n_sample()


def _max_expert_tiles():
    return (2 * _n_tot()) // EXPERT_TILE + _n_experts()


def _deepnorm_alpha():
    return (2 * DEPTH) ** 0.25


def _params(vmem=VMEM_LIMIT):
    return pltpu.CompilerParams(vmem_limit_bytes=vmem)


def _layer_norm(z, g, b):
    mu = jnp.mean(z, axis=-1, keepdims=True)
    zc = z - mu
    var = jnp.mean(zc * zc, axis=-1, keepdims=True)
    return zc * lax.rsqrt(var + LN_EPS) * g + b


def _ada_kernel(c_ref, w_ref, b_ref, o_ref):
    c = c_ref[...]
    s = (c / (1.0 + jnp.exp(-c))).astype(BF16)
    w = w_ref[0].astype(BF16)
    o_ref[0] = jnp.dot(s, w, preferred_element_type=F32) + b_ref[0]


def _ada_params(c_all, ada_w, ada_b):
    r, d = c_all.shape
    n = ada_w.shape[-1]
    tn = min(512, n)
    return pl.pallas_call(
        _ada_kernel,
        grid=(DEPTH, n // tn),
        in_specs=[
            pl.BlockSpec((r, d), lambda l, j: (0, 0)),
            pl.BlockSpec((1, d, tn), lambda l, j: (l, 0, j)),
            pl.BlockSpec((1, 1, tn), lambda l, j: (l, 0, j)),
        ],
        out_specs=pl.BlockSpec((1, r, tn), lambda l, j: (l, 0, j)),
        out_shape=jax.ShapeDtypeStruct((DEPTH, r, n), F32),
        compiler_params=_params(),
        name="ada_params",
    )(c_all, ada_w, ada_b.reshape(DEPTH, 1, n))


def _pool_kernel(x_ref, halo_ref, p_ref, w_ref, vec_ref, *rest, tm, tiles_per_seq, pos0, zero_first, aliased):
    if aliased:
        rest = rest[2:]
    x1_ref, t1_ref, st_ref, hbuf = rest
    i = pl.program_id(0)
    j = i % tiles_per_seq
    sh1 = p_ref[0, 0:1, :]
    sc1 = p_ref[0, 1:2, :]
    g1 = p_ref[0, 2:3, :]
    sh2 = p_ref[0, 3:4, :]
    sc2 = p_ref[0, 4:5, :]
    ls = vec_ref[0:1, :]
    ln_g = vec_ref[1:2, :]
    ln_b = vec_ref[2:3, :]

    x = x_ref[...]
    h = x * (1.0 + sc1) + sh1
    if zero_first:
        hh = halo_ref[...] * (1.0 + sc1) + sh1
        hh = jnp.where(j == 0, 0.0, hh)
    else:
        hh = halo_ref[...]
    hbuf[0:HALO, :] = hh
    hbuf[HALO:, :] = h

    pos = pos0 + j * tm + lax.broadcasted_iota(jnp.int32, (tm, 1), 0)
    gc = D_MODEL // len(POOL_WINDOWS)
    ys = []
    for g, w in enumerate(POOL_WINDOWS):
        s = hbuf[:, g * gc:(g + 1) * gc]
        k = 1
        while k < w:
            s = s + pltpu.roll(s, k, 0)
            k *= 2
        inv = 1.0 / jnp.minimum(pos + 1, w).astype(F32)
        pooled = s[HALO:, :] * inv - hbuf[HALO:, g * gc:(g + 1) * gc]
        ys.append(jnp.dot(pooled.astype(BF16), w_ref[g], preferred_element_type=F32))
    m = jnp.concatenate(ys, axis=-1) * ls
    x1 = _layer_norm(_deepnorm_alpha() * x + (1.0 + g1) * m, ln_g, ln_b)
    x1_ref[...] = x1
    t1_ref[...] = x1 * (1.0 + sc2) + sh2

    @pl.when(j == tiles_per_seq - 1)
    def _():
        st_ref[0] = hbuf[tm:, :]


def _pool_layer(xp, xs, hist_pad, pp, ps, pool_w, vecs):
    d = D_MODEL
    n_tot = _n_tot()
    gc = d // len(POOL_WINDOWS)
    np_tiles = _n_prompt() // ROW_TILE
    tps = SEQ // ROW_TILE
    hb = ROW_TILE // HALO
    common = [
        pl.BlockSpec((len(POOL_WINDOWS), gc, gc), lambda i: (0, 0, 0)),
        pl.BlockSpec((3, d), lambda i: (0, 0)),
    ]
    x1, t1, st_p = pl.pallas_call(
        functools.partial(_pool_kernel, tm=ROW_TILE, tiles_per_seq=tps, pos0=0, zero_first=True, aliased=False),
        grid=(np_tiles,),
        in_specs=[
            pl.BlockSpec((ROW_TILE, d), lambda i: (i, 0)),
            pl.BlockSpec((HALO, d), lambda i: (jnp.maximum(i * hb - 1, 0), 0)),
            pl.BlockSpec((1, 5, d), lambda i: (i // tps, 0, 0)),
        ] + common,
        out_specs=[
            pl.BlockSpec((ROW_TILE, d), lambda i: (i, 0)),
            pl.BlockSpec((ROW_TILE, d), lambda i: (i, 0)),
            pl.BlockSpec((1, HALO, d), lambda i: (i // tps, 0, 0)),
        ],
        out_shape=[
            jax.ShapeDtypeStruct((n_tot, d), F32),
            jax.ShapeDtypeStruct((n_tot, d), F32),
            jax.ShapeDtypeStruct((BATCH, HALO, d), F32),
        ],
        scratch_shapes=[pltpu.VMEM((ROW_TILE + HALO, d), F32)],
        compiler_params=_params(),
        name="pool_prompt",
    )(xp, xp, pp, pool_w, vecs)

    off = _n_prompt() // DEC_SEQ
    x1, t1, h_s = pl.pallas_call(
        functools.partial(_pool_kernel, tm=DEC_SEQ, tiles_per_seq=1, pos0=PAST_LEN, zero_first=False, aliased=True),
        grid=(DEC_BATCH,),
        in_specs=[
            pl.BlockSpec((DEC_SEQ, d), lambda i: (i, 0)),
            pl.BlockSpec((HALO, d), lambda i: (i, 0)),
            pl.BlockSpec((1, 5, d), lambda i: (i, 0, 0)),
        ] + common + [
            pl.BlockSpec(memory_space=pl.ANY),
            pl.BlockSpec(memory_space=pl.ANY),
        ],
        out_specs=[
            pl.BlockSpec((DEC_SEQ, d), lambda i: (off + i, 0)),
            pl.BlockSpec((DEC_SEQ, d), lambda i: (off + i, 0)),
            pl.BlockSpec((1, HALO, d), lambda i: (i, 0, 0)),
        ],
        out_shape=[
            jax.ShapeDtypeStruct((n_tot, d), F32),
            jax.ShapeDtypeStruct((n_tot, d), F32),
            jax.ShapeDtypeStruct((DEC_BATCH, HALO, d), F32),
        ],
        scratch_shapes=[pltpu.VMEM((DEC_SEQ + HALO, d), F32)],
        input_output_aliases={5: 0, 6: 1},
        compiler_params=_params(),
        name="pool_sample",
    )(xs, hist_pad, ps, pool_w, vecs, x1, t1)
    return x1, t1, st_p, h_s


def _router_kernel(t_ref, wr_ref, br_ref, carry_ref, *rest, tm, aliased):
    if aliased:
        rest = rest[2:]
    _route(t_ref[...], wr_ref, br_ref, carry_ref, *rest, tm=tm)


def _route(t, wr_ref, br_ref, carry_ref, ri_ref, rw_ref, cnt_out_ref, cnt, *, tm):
    i = pl.program_id(0)

    @pl.when(i == 0)
    def _():
        cnt[...] = carry_ref[...]

    t_hi = t.astype(BF16)
    t_lo = (t - t_hi.astype(F32)).astype(BF16)
    both = jnp.dot(t_hi, wr_ref[...], preferred_element_type=F32)
    logits = (both[:, :ROUTER_LANES] + jnp.dot(t_lo, wr_ref[:, :ROUTER_LANES], preferred_element_type=F32)
              + both[:, ROUTER_LANES:]) + br_ref[...]
    lane = lax.broadcasted_iota(jnp.int32, (tm, ROUTER_LANES), 1).astype(F32)
    big = float(ROUTER_LANES)
    ninf = -jnp.inf

    lg = jnp.where(lane < N_GROUPS, logits, ninf)
    mg = jnp.max(lg, axis=-1, keepdims=True)
    gidx = jnp.min(jnp.where(lg == mg, lane, big), axis=-1, keepdims=True)
    p_group = 1.0 / jnp.sum(jnp.exp(lg - mg), axis=-1, keepdims=True)

    lo = EXPERT_LANE0 + gidx * EXPERTS_PER_GROUP
    le = jnp.where((lane >= lo) & (lane < lo + EXPERTS_PER_GROUP), logits, ninf)
    m1 = jnp.max(le, axis=-1, keepdims=True)
    i1 = jnp.min(jnp.where(le == m1, lane, big), axis=-1, keepdims=True)
    le2 = jnp.where(lane == i1, ninf, le)
    m2 = jnp.max(le2, axis=-1, keepdims=True)
    i2 = jnp.min(jnp.where(le2 == m2, lane, big), axis=-1, keepdims=True)
    e2 = jnp.exp(m2 - m1)
    w1 = p_group / (1.0 + e2)
    w2 = p_group * e2 / (1.0 + e2)

    hit1 = lane == i1
    hit2 = lane == i2
    onehot = jnp.where(hit1 | hit2, 1.0, 0.0)
    row = lax.broadcasted_iota(jnp.int32, (tm, tm), 0)
    col = lax.broadcasted_iota(jnp.int32, (tm, tm), 1)
    tri = jnp.where(col < row, 1.0, 0.0).astype(BF16)
    before = jnp.dot(tri, onehot.astype(BF16), preferred_element_type=F32) + cnt[...]
    r1 = jnp.sum(jnp.where(hit1, before, 0.0), axis=-1, keepdims=True)
    r2 = jnp.sum(jnp.where(hit2, before, 0.0), axis=-1, keepdims=True)
    cnt[...] = cnt[...] + jnp.sum(onehot, axis=0, keepdims=True)

    zero = jnp.zeros_like(logits)
    ints = jnp.where(lane == 0, i1 - EXPERT_LANE0,
                     jnp.where(lane == 1, i2 - EXPERT_LANE0,
                               jnp.where(lane == 2, r1, jnp.where(lane == 3, r2, zero))))
    ri_ref[...] = ints.astype(jnp.int32)
    rw_ref[...] = jnp.where(lane == 0, w1, jnp.where(lane == 1, w2, zero))
    cnt_out_ref[...] = cnt[...]


def _router(t, wr, br):
    d = D_MODEL
    n_tot = _n_tot()
    outs_shape = [
        jax.ShapeDtypeStruct((n_tot, ROUTER_LANES), jnp.int32),
        jax.ShapeDtypeStruct((n_tot, ROUTER_LANES), F32),
        jax.ShapeDtypeStruct((1, ROUTER_LANES), F32),
    ]
    fixed = [
        pl.BlockSpec((d, 2 * ROUTER_LANES), lambda i: (0, 0)),
        pl.BlockSpec((1, ROUTER_LANES), lambda i: (0, 0)),
        pl.BlockSpec((1, ROUTER_LANES), lambda i: (0, 0)),
    ]
    ri, rw, cnt = pl.pallas_call(
        functools.partial(_router_kernel, tm=ROW_TILE, aliased=False),
        grid=(_n_prompt() // ROW_TILE,),
        in_specs=[pl.BlockSpec((ROW_TILE, d), lambda i: (i, 0))] + fixed,
        out_specs=[
            pl.BlockSpec((ROW_TILE, ROUTER_LANES), lambda i: (i, 0)),
            pl.BlockSpec((ROW_TILE, ROUTER_LANES), lambda i: (i, 0)),
            pl.BlockSpec((1, ROUTER_LANES), lambda i: (0, 0)),
        ],
        out_shape=outs_shape,
        scratch_shapes=[pltpu.VMEM((1, ROUTER_LANES), F32)],
        compiler_params=_params(),
        name="router_prompt",
    )(t, wr, br, jnp.zeros((1, ROUTER_LANES), F32))
    off = _n_prompt() // DEC_SEQ
    ri, rw, cnt = pl.pallas_call(
        functools.partial(_router_kernel, tm=DEC_SEQ, aliased=True),
        grid=(DEC_BATCH,),
        in_specs=[pl.BlockSpec((DEC_SEQ, d), lambda i: (off + i, 0))] + fixed + [
            pl.BlockSpec(memory_space=pl.ANY),
            pl.BlockSpec(memory_space=pl.ANY),
        ],
        out_specs=[
            pl.BlockSpec((DEC_SEQ, ROUTER_LANES), lambda i: (off + i, 0)),
            pl.BlockSpec((DEC_SEQ, ROUTER_LANES), lambda i: (off + i, 0)),
            pl.BlockSpec((1, ROUTER_LANES), lambda i: (0, 0)),
        ],
        out_shape=outs_shape,
        scratch_shapes=[pltpu.VMEM((1, ROUTER_LANES), F32)],
        input_output_aliases={4: 0, 5: 1},
        compiler_params=_params(),
        name="router_sample",
    )(t, wr, br, cnt, ri, rw)
    return ri, rw, cnt


def _n_slots():
    return (_max_expert_tiles() + 2) * EXPERT_TILE


def _slot_kernel(pos_ref, code_ref):
    n_tot = _n_tot()
    n_pairs = 2 * n_tot

    def init(s, c):
        code_ref[s] = n_pairs + s
        return c

    def fill(r, c):
        code_ref[pos_ref[2 * r]] = r
        code_ref[pos_ref[2 * r + 1]] = n_tot + r
        return c

    lax.fori_loop(0, _n_slots(), init, 0, unroll=8)
    lax.fori_loop(0, n_tot, fill, 0, unroll=8)


def _slot_codes(pos_flat):
    return pl.pallas_call(
        _slot_kernel,
        in_specs=[pl.BlockSpec(memory_space=pltpu.SMEM)],
        out_specs=pl.BlockSpec(memory_space=pltpu.SMEM),
        out_shape=jax.ShapeDtypeStruct((_n_slots(),), jnp.int32),
        name="moe_slots",
    )(pos_flat)


def _gmm_kernel(te_ref, nu_ref, src_ref, code_ref, t_hbm, wg_ref, wu_ref, wd_ref, y_hbm,
                xbuf0, xbuf1, ybuf0, ybuf1, gsem, ssem):
    del te_ref
    i = pl.program_id(0)
    nu = nu_ref[0]
    tg = EXPERT_TILE
    warm0 = _max_expert_tiles()
    xbufs = (xbuf0, xbuf1)
    ybufs = (ybuf0, ybuf1)

    def gather(tile, k, r):
        src = src_ref[tile * tg + r]
        return pltpu.make_async_copy(t_hbm.at[pl.ds(src, 1)], xbufs[k].at[pl.ds(r, 1)], gsem.at[k])

    def scatter(tile, k, r):
        dst = code_ref[tile * tg + r]
        return pltpu.make_async_copy(ybufs[k].at[pl.ds(r, 1)], y_hbm.at[pl.ds(dst, 1)], ssem.at[k])

    def wait_gather(k):
        pltpu.make_async_copy(t_hbm.at[pl.ds(0, tg)], xbufs[k], gsem.at[k]).wait()

    def wait_scatter(k):
        pltpu.make_async_copy(ybufs[k], y_hbm.at[pl.ds(0, tg)], ssem.at[k]).wait()

    @pl.when(i == 0)
    def _():
        ybuf0[...] = jnp.zeros_like(ybuf0)
        ybuf1[...] = jnp.zeros_like(ybuf1)
        for r in range(tg):
            gather(0, 0, r).start()
        for r in range(tg):
            scatter(warm0, 0, r).start(priority=SCATTER_PRIORITY)

    def step(k):
        other = 1 - k
        nxt = jnp.minimum(i + 1, nu - 1)
        prev = jnp.where(i == 0, warm0 + 1, i - 1)
        for r in range(tg):
            gather(nxt, other, r).start()
        for r in range(tg):
            scatter(prev, other, r).start(priority=SCATTER_PRIORITY)
        wait_gather(k)
        wait_scatter(k)
        x = xbufs[k][...].astype(BF16)
        a = jnp.dot(x, wg_ref[0, 0], preferred_element_type=F32)
        b = jnp.dot(x, wu_ref[0, 0], preferred_element_type=F32)
        hid = ((a / (1.0 + jnp.exp(-a))) * b).astype(BF16)
        for c in range(D_MODEL // GMM_DOWN_COLS):
            cols = slice(c * GMM_DOWN_COLS, (c + 1) * GMM_DOWN_COLS)
            ybufs[k][:, cols] = jnp.dot(hid, wd_ref[0, 0, :, cols], preferred_element_type=F32)

        @pl.when(i == nu - 1)
        def _():
            for r in range(tg):
                scatter(i, k, r).start(priority=SCATTER_PRIORITY)
            wait_scatter(other)
            wait_scatter(k)
            wait_gather(other)

    for k in range(2):
        @pl.when((i < nu) & (i % 2 == k))
        def _(k=k):
            step(k)


def _gmm(t, code, tile_expert, n_used, layer, wg, wu, wd):
    d = D_MODEL
    tiles = _max_expert_tiles()
    n_tot = _n_tot()
    src = jnp.minimum(jnp.where(code >= n_tot, code - n_tot, code), n_tot - 1)

    def wsel(i, te, nu, src, code):
        return (layer, te[jnp.minimum(i, nu[0] - 1)], 0, 0)

    return pl.pallas_call(
        _gmm_kernel,
        grid_spec=pltpu.PrefetchScalarGridSpec(
            num_scalar_prefetch=4,
            grid=(tiles,),
            in_specs=[
                pl.BlockSpec(memory_space=pl.ANY),
                pl.BlockSpec((1, 1, d, D_EXPERT), wsel),
                pl.BlockSpec((1, 1, d, D_EXPERT), wsel),
                pl.BlockSpec((1, 1, D_EXPERT, d), wsel),
            ],
            out_specs=pl.BlockSpec(memory_space=pl.ANY),
            scratch_shapes=[
                pltpu.VMEM((EXPERT_TILE, d), F32),
                pltpu.VMEM((EXPERT_TILE, d), F32),
                pltpu.VMEM((EXPERT_TILE, d), F32),
                pltpu.VMEM((EXPERT_TILE, d), F32),
                pltpu.SemaphoreType.DMA((2,)),
                pltpu.SemaphoreType.DMA((2,)),
            ],
        ),
        out_shape=jax.ShapeDtypeStruct((2 * _n_tot() + _n_slots(), d), F32),
        compiler_params=_params(),
        name="moe_experts",
    )(tile_expert, n_used, src, code, t, wg, wu, wd)


def _post_kernel(x_ref, y0_ref, y1_ref, rw_ref, p_ref, vec_ref, *rest, emit_next, aliased):
    if aliased:
        rest = rest[(2 if emit_next else 1):]
    g2 = p_ref[0, 0:1, :]
    f = rw_ref[:, 0:1] * y0_ref[...] + rw_ref[:, 1:2] * y1_ref[...]
    xo = _layer_norm(_deepnorm_alpha() * x_ref[...] + (1.0 + g2) * f, vec_ref[0:1, :], vec_ref[1:2, :])
    rest[0][...] = xo
    if emit_next:
        rest[1][...] = (xo * (1.0 + p_ref[0, 2:3, :]) + p_ref[0, 1:2, :]).astype(BF16)


def _post(x, y, rw, pp, ps, vecs, emit_next):
    d = D_MODEL
    n_p, n_tot = _n_prompt(), _n_tot()
    np_tiles = n_p // ROW_TILE
    tps = SEQ // ROW_TILE
    off = n_p // DEC_SEQ
    npar = pp.shape[1]

    def call(tm, grid, roff, par, bmap, prev, name):
        aliased = prev is not None
        y1_off = n_tot // tm
        in_specs = [
            pl.BlockSpec((tm, d), lambda i: (roff + i, 0)),
            pl.BlockSpec((tm, d), lambda i: (roff + i, 0)),
            pl.BlockSpec((tm, d), lambda i: (y1_off + roff + i, 0)),
            pl.BlockSpec((tm, ROUTER_LANES), lambda i: (roff + i, 0)),
            pl.BlockSpec((1, npar, d), lambda i: (bmap(i), 0, 0)),
            pl.BlockSpec((2, d), lambda i: (0, 0)),
        ]
        args = [x, y, y, rw, par, vecs]
        if emit_next:
            oroff, rows = roff, n_tot
        else:
            oroff, rows = 0, grid * tm
        out_specs = [pl.BlockSpec((tm, d), lambda i: (oroff + i, 0))]
        out_shape = [jax.ShapeDtypeStruct((rows, d), F32)]
        if emit_next:
            out_specs.append(pl.BlockSpec((tm, d), lambda i: (oroff + i, 0)))
            out_shape.append(jax.ShapeDtypeStruct((rows, d), BF16))
        aliases = {}
        if aliased:
            for k, a in enumerate(prev):
                in_specs.append(pl.BlockSpec(memory_space=pl.ANY))
                args.append(a)
                aliases[6 + k] = k
        return pl.pallas_call(
            functools.partial(_post_kernel, emit_next=emit_next, aliased=aliased),
            grid=(grid,),
            in_specs=in_specs,
            out_specs=out_specs,
            out_shape=out_shape,
            input_output_aliases=aliases,
            compiler_params=_params(),
            name=name,
        )(*args)

    outs_p = call(ROW_TILE, np_tiles, 0, pp, lambda i: i // tps, None, "moe_post_prompt")
    if emit_next:
        return call(DEC_SEQ, DEC_BATCH, off, ps, lambda i: i, outs_p, "moe_post_sample")
    outs_s = call(DEC_SEQ, DEC_BATCH, off, ps, lambda i: i, None, "moe_post_sample")
    return outs_p[0], outs_s[0]


def _moe(t, x, layer, routed, expert_w, pp, ps, vecs, emit_next):
    ri, rw, cnt = routed
    ne = _n_experts()
    counts = cnt[0, EXPERT_LANE0:EXPERT_LANE0 + ne].astype(jnp.int32)
    tiles_e = (counts + EXPERT_TILE - 1) // EXPERT_TILE
    tile_end = jnp.cumsum(tiles_e)
    pad_off = (tile_end - tiles_e) * EXPERT_TILE
    n_used = tile_end[-1:].astype(jnp.int32)
    tile_ids = jnp.arange(_max_expert_tiles(), dtype=jnp.int32)
    tile_expert = jnp.minimum(jnp.sum(tile_ids[:, None] >= tile_end[None, :], axis=1), ne - 1).astype(jnp.int32)
    eid = ri[:, 0:2]
    rank = ri[:, 2:4]
    onehot = eid[:, :, None] == jnp.arange(ne, dtype=jnp.int32)[None, None, :]
    pos = jnp.sum(jnp.where(onehot, pad_off[None, None, :], 0), axis=-1) + rank
    code = _slot_codes(pos.astype(jnp.int32).reshape(-1))
    y = _gmm(t, code, tile_expert, n_used, layer, *expert_w)
    return _post(x, y, rw, pp, ps, vecs, emit_next)


def _mm_kernel(a_ref, w_ref, *out_refs, keep_tiles):
    acc = jnp.dot(a_ref[...], w_ref[...], preferred_element_type=F32)
    out_refs[0][...] = acc.astype(out_refs[0].dtype)
    if keep_tiles:
        i = pl.program_id(1)
        keep = functools.reduce(jnp.logical_or, [i == k for k in keep_tiles])

        @pl.when(keep)
        def _():
            out_refs[1][...] = acc


def _matmul(a, w, out_dtype, keep_tiles=(), name="matmul"):
    m, k = a.shape
    n_out = w.shape[1]
    tm = ROW_TILE
    tn = min(MM_TN, n_out)
    out_specs = [pl.BlockSpec((tm, tn), lambda j, i: (i, j))]
    out_shape = [jax.ShapeDtypeStruct((m, n_out), out_dtype)]
    if keep_tiles:
        def slot(j, i):
            return (sum((i > b).astype(jnp.int32) for b in keep_tiles), j)

        out_specs.append(pl.BlockSpec((tm, tn), slot))
        out_shape.append(jax.ShapeDtypeStruct((len(keep_tiles) * tm, n_out), F32))
    outs = pl.pallas_call(
        functools.partial(_mm_kernel, keep_tiles=tuple(keep_tiles)),
        grid=(n_out // tn, m // tm),
        in_specs=[
            pl.BlockSpec((tm, k), lambda j, i: (i, 0)),
            pl.BlockSpec((k, tn), lambda j, i: (0, j)),
        ],
        out_specs=out_specs,
        out_shape=out_shape,
        compiler_params=_params(),
        name=name,
    )(a, w)
    return outs


def _lane_half(shape, hh):
    return (lax.broadcasted_iota(jnp.int32, shape, 1) < HEAD_DIM) == (hh == 0)


def _attend_group(problems):
    scores, tops = [], []
    for q2, hh, pieces in problems:
        qm = jnp.where(_lane_half(q2.shape, hh), q2, jnp.zeros_like(q2))
        ss = [lax.dot_general(qm, k, (((1,), (1,)), ((), ())), preferred_element_type=F32) + bias
              for k, _, bias, _ in pieces]
        m = None
        for s, (_, _, _, pen) in zip(ss, pieces):
            rm = jnp.max(s, axis=-1, keepdims=True) + pen
            m = rm if m is None else jnp.maximum(m, rm)
        scores.append(ss)
        tops.append(m)
    outs = []
    for (q2, hh, pieces), ss, m in zip(problems, scores, tops):
        o = None
        for s, (_, v, _, pen) in zip(ss, pieces):
            vm = jnp.where(_lane_half(v.shape, hh), v, jnp.ones_like(v))
            po = jnp.dot(jnp.exp2(s - (m - pen)).astype(BF16), vm, preferred_element_type=F32)
            o = po if o is None else o + po
        outs.append(o)
    return outs


def _merge_heads(o_a, o_b):
    first = _lane_half(o_a.shape, 0)
    o = jnp.where(first, o_a, o_b)
    den = jnp.where(first, pltpu.roll(o_a, HEAD_DIM, 1), pltpu.roll(o_b, HEAD_DIM, 1))
    return o / den


def _attn_prompt_kernel(q_ref, k0_ref, k1_ref, k2_ref, v0_ref, v1_ref, v2_ref, bias_ref, o_ref):
    cg = pl.program_id(2)
    k_refs = (k0_ref, k1_ref, k2_ref)
    v_refs = (v0_ref, v1_ref, v2_ref)
    pens = (jnp.where(cg >= 2, 0.0, NEG_INF), jnp.where(cg >= 1, 0.0, NEG_INF), 0.0)
    qh = ATTN_Q // 2
    n_pairs = ATTN_LANES // (2 * HEAD_DIM)

    def cols(p):
        return slice(p * 2 * HEAD_DIM, (p + 1) * 2 * HEAD_DIM)

    def pieces(p, hh, half):
        out = []
        lo, hi = half * qh, half * qh + ATTN_KW
        for kb in range(3):
            a, b = max(lo, kb * ATTN_Q), min(hi, (kb + 1) * ATTN_Q)
            if a < b:
                rows = slice(a - kb * ATTN_Q, b - kb * ATTN_Q)
                out.append((k_refs[kb][rows, cols(p)], v_refs[kb][rows, cols(p)],
                            bias_ref[2 * p + hh, :, a - lo:b - lo], pens[kb]))
        return out

    for p0 in range(0, n_pairs, ATTN_GROUP_PAIRS):
        keys = [(p, hh, half) for p in range(p0, p0 + ATTN_GROUP_PAIRS) for hh in range(2) for half in range(2)]
        outs = _attend_group([(q_ref[half * qh:(half + 1) * qh, cols(p)], hh, pieces(p, hh, half))
                              for p, hh, half in keys])
        res = dict(zip(keys, outs))
        for p in range(p0, p0 + ATTN_GROUP_PAIRS):
            for half in range(2):
                o_ref[half * qh:(half + 1) * qh, cols(p)] = _merge_heads(
                    res[p, 0, half], res[p, 1, half]).astype(o_ref.dtype)


def _attn_prompt(q, k, v, bias):
    d = D_MODEL
    cgs = SEQ // ATTN_Q
    hpg = ATTN_LANES // HEAD_DIM

    def kmap(back):
        return lambda hg, b, c: (b * cgs + jnp.maximum(c - back, 0), hg)

    blk = (ATTN_Q, ATTN_LANES)
    return pl.pallas_call(
        _attn_prompt_kernel,
        grid=(d // ATTN_LANES, BATCH, cgs),
        in_specs=[
            pl.BlockSpec(blk, lambda hg, b, c: (b * cgs + c, hg)),
            pl.BlockSpec(blk, kmap(2)), pl.BlockSpec(blk, kmap(1)), pl.BlockSpec(blk, kmap(0)),
            pl.BlockSpec(blk, kmap(2)), pl.BlockSpec(blk, kmap(1)), pl.BlockSpec(blk, kmap(0)),
            pl.BlockSpec((hpg, ATTN_Q // 2, ATTN_KW), lambda hg, b, c: (hg, 0, 0)),
        ],
        out_specs=pl.BlockSpec(blk, lambda hg, b, c: (b * cgs + c, hg)),
        out_shape=jax.ShapeDtypeStruct((_n_tot(), d), BF16),
        compiler_params=_params(),
        name="attn_prompt",
    )(q, k, k, k, v, v, v, bias)


def _attn_sample_kernel(q_ref, kn_ref, vn_ref, ck_ref, cv_ref, bc_ref, bn_ref, o_in_ref, o_ref):
    del o_in_ref
    n_pairs = ATTN_LANES // (2 * HEAD_DIM)
    problems = []
    for p in range(n_pairs):
        cols = slice(p * 2 * HEAD_DIM, (p + 1) * 2 * HEAD_DIM)
        kc = ck_ref[0, :, cols].astype(BF16)
        vc = cv_ref[0, :, cols].astype(BF16)
        for hh in range(2):
            problems.append((q_ref[:, cols], hh, [(kc, vc, bc_ref[2 * p + hh], 0.0),
                                                  (kn_ref[:, cols], vn_ref[:, cols], bn_ref[2 * p + hh], 0.0)]))
    outs = _attend_group(problems)
    for p in range(n_pairs):
        cols = slice(p * 2 * HEAD_DIM, (p + 1) * 2 * HEAD_DIM)
        o_ref[:, cols] = _merge_heads(outs[2 * p], outs[2 * p + 1]).astype(o_ref.dtype)


def _attn_sample(q, k, v, ck, cv, bias_c, bias_n, o):
    d = D_MODEL
    off = _n_prompt() // DEC_SEQ
    hpg = ATTN_LANES // HEAD_DIM
    rows = ck.shape[1]
    new = pl.BlockSpec((DEC_SEQ, ATTN_LANES), lambda hg, b: (off + b, hg))
    cache = pl.BlockSpec((1, rows, ATTN_LANES), lambda hg, b: (b, 0, hg))
    return pl.pallas_call(
        _attn_sample_kernel,
        grid=(d // ATTN_LANES, DEC_BATCH),
        in_specs=[
            new, new, new, cache, cache,
            pl.BlockSpec((hpg, DEC_SEQ, rows), lambda hg, b: (hg, 0, 0)),
            pl.BlockSpec((hpg, DEC_SEQ, DEC_SEQ), lambda hg, b: (hg, 0, 0)),
            pl.BlockSpec(memory_space=pl.ANY),
        ],
        out_specs=new,
        out_shape=jax.ShapeDtypeStruct((_n_tot(), d), BF16),
        input_output_aliases={7: 0},
        compiler_params=_params(),
        name="attn_sample",
    )(q, k, v, ck, cv, bias_c, bias_n, o)


def _rel_index(dist):
    return jnp.clip(dist, -REL_CLIP, REL_CLIP) + REL_CLIP


def _attn_biases(rel_bias, cache_rows):
    tbl = rel_bias * LOG2_E
    n_heads = tbl.shape[0]
    nq = ATTN_Q // 2
    kw = ATTN_KW
    back = N_LEFT_CHUNKS * CHUNK
    span = kw + nq - 1
    u = tbl[:, _rel_index(back + (nq - 1) - jnp.arange(span))]
    rep = jnp.broadcast_to(jnp.pad(u, ((0, 0), (0, 1)))[:, None, :], (n_heads, nq, span + 1))
    toep = rep.reshape(n_heads, nq * (span + 1))[:, :nq * span].reshape(n_heads, nq, span)
    qi = jnp.arange(nq)[:, None]
    kj = jnp.arange(kw)[None, :]
    qc = qi // CHUNK
    kc = kj // CHUNK - back // CHUNK
    vis = (kc <= qc) & (kc >= qc - N_LEFT_CHUNKS)
    bias_p = jnp.where(vis[None], toep[:, :, nq - 1:], NEG_INF)
    si = jnp.arange(DEC_SEQ)[:, None]
    dist_c = si + cache_rows - jnp.arange(cache_rows)[None, :]
    dist_n = si - jnp.arange(DEC_SEQ)[None, :]
    return bias_p, tbl[:, _rel_index(dist_c)], tbl[:, _rel_index(dist_n)]


def _ln1_kernel(x_ref, m_ref, p_ref, vec_ref, wr_ref, br_ref, carry_ref, *rest, tm, aliased):
    if aliased:
        rest = rest[4:]
    x1_ref, t1_ref, ri_ref, rw_ref, cnt_out_ref, cnt = rest
    g1 = p_ref[0, 0:1, :]
    x1 = _layer_norm(_deepnorm_alpha() * x_ref[...] + (1.0 + g1) * m_ref[...], vec_ref[0:1, :], vec_ref[1:2, :])
    x1_ref[...] = x1
    t1 = x1 * (1.0 + p_ref[0, 2:3, :]) + p_ref[0, 1:2, :]
    t1_ref[...] = t1
    _route(t1, wr_ref, br_ref, carry_ref, ri_ref, rw_ref, cnt_out_ref, cnt, tm=tm)


def _ln1(x, m, pp, ps, vecs, wr, br):
    d = D_MODEL
    n_tot = _n_tot()
    tps = SEQ // ROW_TILE
    off = _n_prompt() // DEC_SEQ
    out_shape = [
        jax.ShapeDtypeStruct((n_tot, d), F32),
        jax.ShapeDtypeStruct((n_tot, d), F32),
        jax.ShapeDtypeStruct((n_tot, ROUTER_LANES), jnp.int32),
        jax.ShapeDtypeStruct((n_tot, ROUTER_LANES), F32),
        jax.ShapeDtypeStruct((1, ROUTER_LANES), F32),
    ]

    def call(tm, grid, roff, par, bmap, carry, prev, name):
        blk = pl.BlockSpec((tm, d), lambda i: (roff + i, 0))
        rblk = pl.BlockSpec((tm, ROUTER_LANES), lambda i: (roff + i, 0))
        one = pl.BlockSpec((1, ROUTER_LANES), lambda i: (0, 0))
        in_specs = [blk, blk, pl.BlockSpec((1, 3, d), lambda i: (bmap(i), 0, 0)), pl.BlockSpec((2, d), lambda i: (0, 0)),
                    pl.BlockSpec((d, 2 * ROUTER_LANES), lambda i: (0, 0)), one, one]
        args = [x, m, par, vecs, wr, br, carry]
        aliases = {}
        if prev is not None:
            in_specs += [pl.BlockSpec(memory_space=pl.ANY)] * 4
            args += list(prev)
            aliases = {7: 0, 8: 1, 9: 2, 10: 3}
        return pl.pallas_call(
            functools.partial(_ln1_kernel, tm=tm, aliased=prev is not None),
            grid=(grid,),
            in_specs=in_specs,
            out_specs=[blk, blk, rblk, rblk, one],
            out_shape=out_shape,
            scratch_shapes=[pltpu.VMEM((1, ROUTER_LANES), F32)],
            input_output_aliases=aliases,
            compiler_params=_params(),
            name=name,
        )(*args)

    x1, t1, ri, rw, cnt = call(ROW_TILE, _n_prompt() // ROW_TILE, 0, pp, lambda i: i // tps,
                               jnp.zeros((1, ROUTER_LANES), F32), None, "attn_post_prompt")
    x1, t1, ri, rw, cnt = call(DEC_SEQ, DEC_BATCH, off, ps, lambda i: i, cnt, (x1, t1, ri, rw), "attn_post_sample")
    return x1, t1, (ri, rw, cnt)


def _split_mods(mods_l):
    return jnp.split(mods_l, 6, axis=-1)


def _stack(vectors):
    st = jnp.stack(vectors, axis=1)
    return st[:BATCH], st[BATCH:BATCH + DEC_BATCH]


def _router_weights(rg_w, rg_b, re_w, re_b):
    d = D_MODEL
    ne = _n_experts()
    wr = jnp.zeros((d, ROUTER_LANES), F32)
    wr = wr.at[:, :N_GROUPS].set(rg_w)
    wr = wr.at[:, EXPERT_LANE0:EXPERT_LANE0 + ne].set(jnp.transpose(re_w, (1, 0, 2)).reshape(d, ne))
    br = jnp.zeros((1, ROUTER_LANES), F32)
    br = br.at[0, :N_GROUPS].set(rg_b)
    br = br.at[0, EXPERT_LANE0:EXPERT_LANE0 + ne].set(re_b.reshape(ne))
    wr_hi = wr.astype(BF16)
    wr_lo = (wr - wr_hi.astype(F32)).astype(BF16)
    return jnp.concatenate([wr_hi, wr_lo], axis=1), br


def kernel(x_prompt, x_sample, c_prompt, c_sample, state_pool, cache_k, cache_v, ada_w, ada_b, ln_g, ln_b,
           pool_w, pool_scale, attn_w_qkv, attn_w_o, attn_rel_bias, router_group_w, router_group_b,
           router_expert_w, router_expert_b, moe_w_gate, moe_w_up, moe_w_down):
    d = D_MODEL
    n_p = _n_prompt()
    xp = x_prompt.reshape(n_p, d)
    xs = x_sample.reshape(_n_sample(), d)

    n_c = BATCH + DEC_BATCH
    c_rows = 32
    c_all = jnp.concatenate([c_prompt, c_sample, jnp.zeros((c_rows - n_c, d), F32)], axis=0)
    mods = _ada_params(c_all, ada_w, ada_b)
    sh1_0, sc1_0, g1_0, sh2_0, sc2_0, g2_0 = _split_mods(mods[0])
    sh1_1, sc1_1, g1_1, sh2_1, sc2_1, g2_1 = _split_mods(mods[1])

    expert_w = (moe_w_gate.astype(BF16), moe_w_up.astype(BF16), moe_w_down.astype(BF16))

    def router_w(l):
        return _router_weights(router_group_w[l], router_group_b[l], router_expert_w[l], router_expert_b[l])

    hist_pad = jnp.pad(state_pool[0], ((0, 0), (HALO - POOL_STATE, 0), (0, 0))).reshape(DEC_BATCH * HALO, d)
    pp, ps = _stack([sh1_0, sc1_0, g1_0, sh2_0, sc2_0])
    vecs = jnp.stack([pool_scale[0], ln_g[0, 0], ln_b[0, 0]])
    x1, t1, st_p, h_s = _pool_layer(xp, xs, hist_pad, pp, ps, pool_w[0].astype(BF16), vecs)
    pp, ps = _stack([g2_0, sh1_1, sc1_1])
    x2, a2 = _moe(t1, x1, 0, _router(t1, *router_w(0)), expert_w, pp, ps,
                  jnp.stack([ln_g[0, 1], ln_b[0, 1]]), True)

    tiles_per_seq = SEQ // ROW_TILE
    keep_rows = min(N_LEFT_CHUNKS * CHUNK, SEQ)
    kt = keep_rows // ROW_TILE
    keep_tiles = [b * tiles_per_seq + tiles_per_seq - kt + r for b in range(BATCH) for r in range(kt)]
    keep_tiles.append(n_p // ROW_TILE)
    w_qkv = attn_w_qkv[0]
    w_q = (w_qkv[:, :d] * (HEAD_DIM ** -0.5 * LOG2_E)).astype(BF16)
    (q,) = _matmul(a2, w_q, BF16, name="attn_q")
    k, k_keep = _matmul(a2, w_qkv[:, d:2 * d].astype(BF16), BF16, keep_tiles, name="attn_k")
    v, v_keep = _matmul(a2, w_qkv[:, 2 * d:].astype(BF16), BF16, keep_tiles, name="attn_v")
    cache_rows = cache_k.shape[2]
    bias_p, bias_c, bias_n = _attn_biases(attn_rel_bias[0], cache_rows)
    o = _attn_prompt(q, k, v, bias_p)
    o = _attn_sample(q, k, v, cache_k[0].reshape(DEC_BATCH, cache_rows, d),
                     cache_v[0].reshape(DEC_BATCH, cache_rows, d), bias_c, bias_n, o)
    (m,) = _matmul(o, attn_w_o[0].astype(BF16), F32, name="attn_out")
    pp, ps = _stack([g1_1, sh2_1, sc2_1])
    x3, t3, routed = _ln1(x2, m, pp, ps, jnp.stack([ln_g[1, 0], ln_b[1, 0]]), *router_w(1))
    pp, ps = _stack([g2_1])
    yp, ys = _moe(t3, x3, 1, routed, expert_w, pp, ps, jnp.stack([ln_g[1, 1], ln_b[1, 1]]), False)

    n_heads = d // HEAD_DIM
    kp_rows = BATCH * keep_rows

    def prompt_cache(keep):
        return keep[:kp_rows].reshape(1, BATCH, keep_rows, n_heads, HEAD_DIM)

    def sample_cache(keep):
        return keep[kp_rows:kp_rows + _n_sample()].reshape(1, DEC_BATCH, DEC_SEQ, n_heads, HEAD_DIM)

    return (
        yp.reshape(BATCH, SEQ, d),
        ys.reshape(DEC_BATCH, DEC_SEQ, d),
        st_p[:, HALO - POOL_STATE:][None],
        prompt_cache(k_keep),
        prompt_cache(v_keep),
        h_s[:, HALO - POOL_STATE:][None],
        sample_cache(k_keep),
        sample_cache(v_keep),
    )
```

```python
import functools
import math

import jax
import jax.numpy as jnp
from jax import lax
from jax.experimental import pallas as pl
from jax.experimental.pallas import tpu as pltpu

D_MODEL = 4096
BATCH = 2
SEQ = 8192
DEPTH = 2
DEC_BATCH = 16
DEC_SEQ = 16
PAST_LEN = 4096
CHUNK = 64
POOL_WINDOWS = (2, 4, 8, 16)
POOL_STATE = 15
HEAD_DIM = 64
N_LEFT_CHUNKS = 8
REL_CLIP = 256
N_GROUPS = 4
EXPERTS_PER_GROUP = 8
D_EXPERT = 512
LN_EPS = 1e-5
NEG_INF = -1e30

ROW_TILE = 256
HALO = 16
EXPERT_TILE = 256
GMM_DOWN_COLS = 512
SCATTER_PRIORITY = 1
MM_TN = 2048
ATTN_Q = 256
ATTN_LANES = 2048
ATTN_GROUP_PAIRS = 8
ATTN_KW = ATTN_Q // 2 + N_LEFT_CHUNKS * CHUNK
ROUTER_LANES = 128
EXPERT_LANE0 = 32
VMEM_LIMIT = 56 * 1024 * 1024
LOG2_E = math.log2(math.e)

F32 = jnp.float32
BF16 = jnp.bfloat16


def _n_experts():
    return N_GROUPS * EXPERTS_PER_GROUP


def _n_prompt():
    return BATCH * SEQ


def _n_sample():
    return DEC_BATCH * DEC_SEQ


def _n_tot():
    return _n_prompt() + _n_sample()


def _max_expert_tiles():
    return (2 * _n_tot()) // EXPERT_TILE + _n_experts()


def _deepnorm_alpha():
    return (2 * DEPTH) ** 0.25


def _params(vmem=VMEM_LIMIT):
    return pltpu.CompilerParams(vmem_limit_bytes=vmem)


def _layer_norm(z, g, b):
    mu = jnp.mean(z, axis=-1, keepdims=True)
    zc = z - mu
    var = jnp.mean(zc * zc, axis=-1, keepdims=True)
    return zc * lax.rsqrt(var + LN_EPS) * g + b


def _ada_kernel(c_ref, w_ref, b_ref, o_ref):
    c = c_ref[...]
    s = (c / (1.0 + jnp.exp(-c))).astype(BF16)
    w = w_ref[0].astype(BF16)
    o_ref[0] = jnp.dot(s, w, preferred_element_type=F32) + b_ref[0]


def _ada_params(c_all, ada_w, ada_b):
    r, d = c_all.shape
    n = ada_w.shape[-1]
    tn = min(1024, n)
    return pl.pallas_call(
        _ada_kernel,
        grid=(DEPTH, n // tn),
        in_specs=[
            pl.BlockSpec((r, d), lambda l, j: (0, 0)),
            pl.BlockSpec((1, d, tn), lambda l, j: (l, 0, j)),
            pl.BlockSpec((1, 1, tn), lambda l, j: (l, 0, j)),
        ],
        out_specs=pl.BlockSpec((1, r, tn), lambda l, j: (l, 0, j)),
        out_shape=jax.ShapeDtypeStruct((DEPTH, r, n), F32),
        compiler_params=_params(),
        name="ada_params",
    )(c_all, ada_w, ada_b.reshape(DEPTH, 1, n))


def _pool_kernel(x_ref, halo_ref, p_ref, w_ref, vec_ref, *rest, tm, tiles_per_seq, pos0, zero_first, aliased):
    if aliased:
        rest = rest[2:]
    x1_ref, t1_ref, st_ref, hbuf = rest
    i = pl.program_id(0)
    j = i % tiles_per_seq
    sh1 = p_ref[0, 0:1, :]
    sc1 = p_ref[0, 1:2, :]
    g1 = p_ref[0, 2:3, :]
    sh2 = p_ref[0, 3:4, :]
    sc2 = p_ref[0, 4:5, :]
    ls = vec_ref[0:1, :]
    ln_g = vec_ref[1:2, :]
    ln_b = vec_ref[2:3, :]

    x = x_ref[...]
    h = x * (1.0 + sc1) + sh1
    if zero_first:
        hh = halo_ref[...] * (1.0 + sc1) + sh1
        hh = jnp.where(j == 0, 0.0, hh)
    else:
        hh = halo_ref[...]
    hbuf[0:HALO, :] = hh
    hbuf[HALO:, :] = h

    pos = pos0 + j * tm + lax.broadcasted_iota(jnp.int32, (tm, 1), 0)
    gc = D_MODEL // len(POOL_WINDOWS)
    ys = []
    for g, w in enumerate(POOL_WINDOWS):
        s = hbuf[:, g * gc:(g + 1) * gc]
        k = 1
        while k < w:
            s = s + pltpu.roll(s, k, 0)
            k *= 2
        inv = 1.0 / jnp.minimum(pos + 1, w).astype(F32)
        pooled = s[HALO:, :] * inv - hbuf[HALO:, g * gc:(g + 1) * gc]
        ys.append(jnp.dot(pooled.astype(BF16), w_ref[g], preferred_element_type=F32))
    m = jnp.concatenate(ys, axis=-1)
    x1 = _layer_norm(_deepnorm_alpha() * x + ((1.0 + g1) * ls) * m, ln_g, ln_b)
    x1_ref[...] = x1
    t1_ref[...] = x1 * (1.0 + sc2) + sh2

    @pl.when(j == tiles_per_seq - 1)
    def _():
        st_ref[0] = hbuf[tm:, :]


def _pool_layer(xp, xs, hist_pad, pp, ps, pool_w, vecs):
    d = D_MODEL
    n_tot = _n_tot()
    gc = d // len(POOL_WINDOWS)
    np_tiles = _n_prompt() // ROW_TILE
    tps = SEQ // ROW_TILE
    hb = ROW_TILE // HALO
    common = [
        pl.BlockSpec((len(POOL_WINDOWS), gc, gc), lambda i: (0, 0, 0)),
        pl.BlockSpec((3, d), lambda i: (0, 0)),
    ]
    x1, t1, st_p = pl.pallas_call(
        functools.partial(_pool_kernel, tm=ROW_TILE, tiles_per_seq=tps, pos0=0, zero_first=True, aliased=False),
        grid=(np_tiles,),
        in_specs=[
            pl.BlockSpec((ROW_TILE, d), lambda i: (i, 0)),
            pl.BlockSpec((HALO, d), lambda i: (jnp.maximum(i * hb - 1, 0), 0)),
            pl.BlockSpec((1, 5, d), lambda i: (i // tps, 0, 0)),
        ] + common,
        out_specs=[
            pl.BlockSpec((ROW_TILE, d), lambda i: (i, 0)),
            pl.BlockSpec((ROW_TILE, d), lambda i: (i, 0)),
            pl.BlockSpec((1, HALO, d), lambda i: (i // tps, 0, 0)),
        ],
        out_shape=[
            jax.ShapeDtypeStruct((n_tot, d), F32),
            jax.ShapeDtypeStruct((n_tot, d), F32),
            jax.ShapeDtypeStruct((BATCH, HALO, d), F32),
        ],
        scratch_shapes=[pltpu.VMEM((ROW_TILE + HALO, d), F32)],
        compiler_params=_params(),
        name="pool_prompt",
    )(xp, xp, pp, pool_w, vecs)

    off = _n_prompt() // DEC_SEQ
    x1, t1, h_s = pl.pallas_call(
        functools.partial(_pool_kernel, tm=DEC_SEQ, tiles_per_seq=1, pos0=PAST_LEN, zero_first=False, aliased=True),
        grid=(DEC_BATCH,),
        in_specs=[
            pl.BlockSpec((DEC_SEQ, d), lambda i: (i, 0)),
            pl.BlockSpec((HALO, d), lambda i: (i, 0)),
            pl.BlockSpec((1, 5, d), lambda i: (i, 0, 0)),
        ] + common + [
            pl.BlockSpec(memory_space=pl.ANY),
            pl.BlockSpec(memory_space=pl.ANY),
        ],
        out_specs=[
            pl.BlockSpec((DEC_SEQ, d), lambda i: (off + i, 0)),
            pl.BlockSpec((DEC_SEQ, d), lambda i: (off + i, 0)),
            pl.BlockSpec((1, HALO, d), lambda i: (i, 0, 0)),
        ],
        out_shape=[
            jax.ShapeDtypeStruct((n_tot, d), F32),
            jax.ShapeDtypeStruct((n_tot, d), F32),
            jax.ShapeDtypeStruct((DEC_BATCH, HALO, d), F32),
        ],
        scratch_shapes=[pltpu.VMEM((DEC_SEQ + HALO, d), F32)],
        input_output_aliases={5: 0, 6: 1},
        compiler_params=_params(),
        name="pool_sample",
    )(xs, hist_pad, ps, pool_w, vecs, x1, t1)
    return x1, t1, st_p, h_s


def _router_kernel(t_ref, wr_ref, br_ref, carry_ref, *rest, tm, aliased):
    if aliased:
        rest = rest[2:]
    _route(t_ref[...], wr_ref, br_ref, carry_ref, *rest, tm=tm)


def _route(t, wr_ref, br_ref, carry_ref, ri_ref, rw_ref, cnt_out_ref, cnt, *, tm):
    i = pl.program_id(0)

    @pl.when(i == 0)
    def _():
        cnt[...] = carry_ref[...]

    t_hi = t.astype(BF16)
    t_lo = (t - t_hi.astype(F32)).astype(BF16)
    both = jnp.dot(t_hi, wr_ref[...], preferred_element_type=F32)
    logits = (both[:, :ROUTER_LANES] + jnp.dot(t_lo, wr_ref[:, :ROUTER_LANES], preferred_element_type=F32)
              + both[:, ROUTER_LANES:]) + br_ref[...]
    lane = lax.broadcasted_iota(jnp.int32, (tm, ROUTER_LANES), 1).astype(F32)
    big = float(ROUTER_LANES)
    ninf = -jnp.inf

    lg = jnp.where(lane < N_GROUPS, logits, ninf)
    mg = jnp.max(lg, axis=-1, keepdims=True)
    gidx = jnp.min(jnp.where(lg == mg, lane, big), axis=-1, keepdims=True)
    p_group = 1.0 / jnp.sum(jnp.exp(lg - mg), axis=-1, keepdims=True)

    lo = EXPERT_LANE0 + gidx * EXPERTS_PER_GROUP
    le = jnp.where((lane >= lo) & (lane < lo + EXPERTS_PER_GROUP), logits, ninf)
    m1 = jnp.max(le, axis=-1, keepdims=True)
    i1 = jnp.min(jnp.where(le == m1, lane, big), axis=-1, keepdims=True)
    le2 = jnp.where(lane == i1, ninf, le)
    m2 = jnp.max(le2, axis=-1, keepdims=True)
    i2 = jnp.min(jnp.where(le2 == m2, lane, big), axis=-1, keepdims=True)
    e2 = jnp.exp(m2 - m1)
    w1 = p_group / (1.0 + e2)
    w2 = p_group * e2 / (1.0 + e2)

    hit1 = lane == i1
    hit2 = lane == i2
    onehot = jnp.where(hit1 | hit2, 1.0, 0.0)
    row = lax.broadcasted_iota(jnp.int32, (tm, tm), 0)
    col = lax.broadcasted_iota(jnp.int32, (tm, tm), 1)
    tri = jnp.where(col < row, 1.0, 0.0).astype(BF16)
    before = jnp.dot(tri, onehot.astype(BF16), preferred_element_type=F32) + cnt[...]
    r1 = jnp.sum(jnp.where(hit1, before, 0.0), axis=-1, keepdims=True)
    r2 = jnp.sum(jnp.where(hit2, before, 0.0), axis=-1, keepdims=True)
    cnt[...] = cnt[...] + jnp.sum(onehot, axis=0, keepdims=True)

    zero = jnp.zeros_like(logits)
    ints = jnp.where(lane == 0, i1 - EXPERT_LANE0,
                     jnp.where(lane == 1, i2 - EXPERT_LANE0,
                               jnp.where(lane == 2, r1, jnp.where(lane == 3, r2, zero))))
    ri_ref[...] = ints.astype(jnp.int32)
    rw_ref[...] = jnp.where(lane == 0, w1, jnp.where(lane == 1, w2, zero))
    cnt_out_ref[...] = cnt[...]


def _router(t, wr, br):
    d = D_MODEL
    n_tot = _n_tot()
    outs_shape = [
        jax.ShapeDtypeStruct((n_tot, ROUTER_LANES), jnp.int32),
        jax.ShapeDtypeStruct((n_tot, ROUTER_LANES), F32),
        jax.ShapeDtypeStruct((1, ROUTER_LANES), F32),
    ]
    fixed = [
        pl.BlockSpec((d, 2 * ROUTER_LANES), lambda i: (0, 0)),
        pl.BlockSpec((1, ROUTER_LANES), lambda i: (0, 0)),
        pl.BlockSpec((1, ROUTER_LANES), lambda i: (0, 0)),
    ]
    ri, rw, cnt = pl.pallas_call(
        functools.partial(_router_kernel, tm=ROW_TILE, aliased=False),
        grid=(_n_prompt() // ROW_TILE,),
        in_specs=[pl.BlockSpec((ROW_TILE, d), lambda i: (i, 0))] + fixed,
        out_specs=[
            pl.BlockSpec((ROW_TILE, ROUTER_LANES), lambda i: (i, 0)),
            pl.BlockSpec((ROW_TILE, ROUTER_LANES), lambda i: (i, 0)),
            pl.BlockSpec((1, ROUTER_LANES), lambda i: (0, 0)),
        ],
        out_shape=outs_shape,
        scratch_shapes=[pltpu.VMEM((1, ROUTER_LANES), F32)],
        compiler_params=_params(),
        name="router_prompt",
    )(t, wr, br, jnp.zeros((1, ROUTER_LANES), F32))
    off = _n_prompt() // DEC_SEQ
    ri, rw, cnt = pl.pallas_call(
        functools.partial(_router_kernel, tm=DEC_SEQ, aliased=True),
        grid=(DEC_BATCH,),
        in_specs=[pl.BlockSpec((DEC_SEQ, d), lambda i: (off + i, 0))] + fixed + [
            pl.BlockSpec(memory_space=pl.ANY),
            pl.BlockSpec(memory_space=pl.ANY),
        ],
        out_specs=[
            pl.BlockSpec((DEC_SEQ, ROUTER_LANES), lambda i: (off + i, 0)),
            pl.BlockSpec((DEC_SEQ, ROUTER_LANES), lambda i: (off + i, 0)),
            pl.BlockSpec((1, ROUTER_LANES), lambda i: (0, 0)),
        ],
        out_shape=outs_shape,
        scratch_shapes=[pltpu.VMEM((1, ROUTER_LANES), F32)],
        input_output_aliases={4: 0, 5: 1},
        compiler_params=_params(),
        name="router_sample",
    )(t, wr, br, cnt, ri, rw)
    return ri, rw, cnt


def _n_slots():
    return (_max_expert_tiles() + 2) * EXPERT_TILE


def _slot_kernel(pos_ref, code_ref):
    n_tot = _n_tot()
    n_pairs = 2 * n_tot

    def init(s, c):
        code_ref[s] = n_pairs + s
        return c

    def fill(r, c):
        code_ref[pos_ref[2 * r]] = r
        code_ref[pos_ref[2 * r + 1]] = n_tot + r
        return c

    lax.fori_loop(0, _n_slots(), init, 0, unroll=8)
    lax.fori_loop(0, n_tot, fill, 0, unroll=8)


def _slot_codes(pos_flat):
    return pl.pallas_call(
        _slot_kernel,
        in_specs=[pl.BlockSpec(memory_space=pltpu.SMEM)],
        out_specs=pl.BlockSpec(memory_space=pltpu.SMEM),
        out_shape=jax.ShapeDtypeStruct((_n_slots(),), jnp.int32),
        name="moe_slots",
    )(pos_flat)


def _gmm_kernel(te_ref, nu_ref, src_ref, code_ref, t_hbm, wg_ref, wu_ref, wd_ref, y_hbm,
                xbuf0, xbuf1, ybuf0, ybuf1, gsem, ssem):
    del te_ref
    i = pl.program_id(0)
    nu = nu_ref[0]
    tg = EXPERT_TILE
    warm0 = _max_expert_tiles()
    xbufs = (xbuf0, xbuf1)
    ybufs = (ybuf0, ybuf1)

    def gather(tile, k, r):
        src = src_ref[tile * tg + r]
        return pltpu.make_async_copy(t_hbm.at[pl.ds(src, 1)], xbufs[k].at[pl.ds(r, 1)], gsem.at[k])

    def scatter(tile, k, r):
        dst = code_ref[tile * tg + r]
        return pltpu.make_async_copy(ybufs[k].at[pl.ds(r, 1)], y_hbm.at[pl.ds(dst, 1)], ssem.at[k])

    def wait_gather(k):
        pltpu.make_async_copy(t_hbm.at[pl.ds(0, tg)], xbufs[k], gsem.at[k]).wait()

    def wait_scatter(k):
        pltpu.make_async_copy(ybufs[k], y_hbm.at[pl.ds(0, tg)], ssem.at[k]).wait()

    @pl.when(i == 0)
    def _():
        ybuf0[...] = jnp.zeros_like(ybuf0)
        ybuf1[...] = jnp.zeros_like(ybuf1)
        for r in range(tg):
            gather(0, 0, r).start()
        for r in range(tg):
            scatter(warm0, 0, r).start(priority=SCATTER_PRIORITY)

    def step(k):
        other = 1 - k
        nxt = jnp.minimum(i + 1, nu - 1)
        prev = jnp.where(i == 0, warm0 + 1, i - 1)
        for r in range(tg):
            gather(nxt, other, r).start()
        for r in range(tg):
            scatter(prev, other, r).start(priority=SCATTER_PRIORITY)
        wait_gather(k)
        wait_scatter(k)
        x = xbufs[k][...].astype(BF16)
        a = jnp.dot(x, wg_ref[0, 0], preferred_element_type=F32)
        b = jnp.dot(x, wu_ref[0, 0], preferred_element_type=F32)
        hid = ((a / (1.0 + jnp.exp(-a))) * b).astype(BF16)
        for c in range(D_MODEL // GMM_DOWN_COLS):
            cols = slice(c * GMM_DOWN_COLS, (c + 1) * GMM_DOWN_COLS)
            ybufs[k][:, cols] = jnp.dot(hid, wd_ref[0, 0, :, cols], preferred_element_type=F32)

        @pl.when(i == nu - 1)
        def _():
            for r in range(tg):
                scatter(i, k, r).start(priority=SCATTER_PRIORITY)
            wait_scatter(other)
            wait_scatter(k)
            wait_gather(other)

    for k in range(2):
        @pl.when((i < nu) & (i % 2 == k))
        def _(k=k):
            step(k)


def _gmm(t, code, tile_expert, n_used, layer, wg, wu, wd):
    d = D_MODEL
    tiles = _max_expert_tiles()
    n_tot = _n_tot()
    src = jnp.minimum(jnp.where(code >= n_tot, code - n_tot, code), n_tot - 1)

    def wsel(i, te, nu, src, code):
        return (layer, te[jnp.minimum(i, nu[0] - 1)], 0, 0)

    return pl.pallas_call(
        _gmm_kernel,
        grid_spec=pltpu.PrefetchScalarGridSpec(
            num_scalar_prefetch=4,
            grid=(tiles,),
            in_specs=[
                pl.BlockSpec(memory_space=pl.ANY),
                pl.BlockSpec((1, 1, d, D_EXPERT), wsel),
                pl.BlockSpec((1, 1, d, D_EXPERT), wsel),
                pl.BlockSpec((1, 1, D_EXPERT, d), wsel),
            ],
            out_specs=pl.BlockSpec(memory_space=pl.ANY),
            scratch_shapes=[
                pltpu.VMEM((EXPERT_TILE, d), F32),
                pltpu.VMEM((EXPERT_TILE, d), F32),
                pltpu.VMEM((EXPERT_TILE, d), F32),
                pltpu.VMEM((EXPERT_TILE, d), F32),
                pltpu.SemaphoreType.DMA((2,)),
                pltpu.SemaphoreType.DMA((2,)),
            ],
        ),
        out_shape=jax.ShapeDtypeStruct((2 * _n_tot() + _n_slots(), d), F32),
        compiler_params=_params(),
        name="moe_experts",
    )(tile_expert, n_used, src, code, t, wg, wu, wd)


def _post_kernel(x_ref, y0_ref, y1_ref, rw_ref, p_ref, vec_ref, *rest, emit_next, aliased):
    if aliased:
        rest = rest[(2 if emit_next else 1):]
    g2 = p_ref[0, 0:1, :]
    f = rw_ref[:, 0:1] * y0_ref[...] + rw_ref[:, 1:2] * y1_ref[...]
    xo = _layer_norm(_deepnorm_alpha() * x_ref[...] + (1.0 + g2) * f, vec_ref[0:1, :], vec_ref[1:2, :])
    rest[0][...] = xo
    if emit_next:
        rest[1][...] = (xo * (1.0 + p_ref[0, 2:3, :]) + p_ref[0, 1:2, :]).astype(BF16)


def _post(x, y, rw, pp, ps, vecs, emit_next):
    d = D_MODEL
    n_p, n_tot = _n_prompt(), _n_tot()
    np_tiles = n_p // ROW_TILE
    tps = SEQ // ROW_TILE
    off = n_p // DEC_SEQ
    npar = pp.shape[1]

    def call(tm, grid, roff, par, bmap, prev, name):
        aliased = prev is not None
        y1_off = n_tot // tm
        in_specs = [
            pl.BlockSpec((tm, d), lambda i: (roff + i, 0)),
            pl.BlockSpec((tm, d), lambda i: (roff + i, 0)),
            pl.BlockSpec((tm, d), lambda i: (y1_off + roff + i, 0)),
            pl.BlockSpec((tm, ROUTER_LANES), lambda i: (roff + i, 0)),
            pl.BlockSpec((1, npar, d), lambda i: (bmap(i), 0, 0)),
            pl.BlockSpec((2, d), lambda i: (0, 0)),
        ]
        args = [x, y, y, rw, par, vecs]
        if emit_next:
            oroff, rows = roff, n_tot
        else:
            oroff, rows = 0, grid * tm
        out_specs = [pl.BlockSpec((tm, d), lambda i: (oroff + i, 0))]
        out_shape = [jax.ShapeDtypeStruct((rows, d), F32)]
        if emit_next:
            out_specs.append(pl.BlockSpec((tm, d), lambda i: (oroff + i, 0)))
            out_shape.append(jax.ShapeDtypeStruct((rows, d), BF16))
        aliases = {}
        if aliased:
            for k, a in enumerate(prev):
                in_specs.append(pl.BlockSpec(memory_space=pl.ANY))
                args.append(a)
                aliases[6 + k] = k
        return pl.pallas_call(
            functools.partial(_post_kernel, emit_next=emit_next, aliased=aliased),
            grid=(grid,),
            in_specs=in_specs,
            out_specs=out_specs,
            out_shape=out_shape,
            input_output_aliases=aliases,
            compiler_params=_params(),
            name=name,
        )(*args)

    outs_p = call(ROW_TILE, np_tiles, 0, pp, lambda i: i // tps, None, "moe_post_prompt")
    if emit_next:
        return call(DEC_SEQ, DEC_BATCH, off, ps, lambda i: i, outs_p, "moe_post_sample")
    outs_s = call(DEC_SEQ, DEC_BATCH, off, ps, lambda i: i, None, "moe_post_sample")
    return outs_p[0], outs_s[0]


def _moe(t, x, layer, routed, expert_w, pp, ps, vecs, emit_next):
    ri, rw, cnt = routed
    ne = _n_experts()
    counts = cnt[0, EXPERT_LANE0:EXPERT_LANE0 + ne].astype(jnp.int32)
    tiles_e = (counts + EXPERT_TILE - 1) // EXPERT_TILE
    tile_end = jnp.cumsum(tiles_e)
    pad_off = (tile_end - tiles_e) * EXPERT_TILE
    n_used = tile_end[-1:].astype(jnp.int32)
    tile_ids = jnp.arange(_max_expert_tiles(), dtype=jnp.int32)
    tile_expert = jnp.minimum(jnp.sum(tile_ids[:, None] >= tile_end[None, :], axis=1), ne - 1).astype(jnp.int32)
    eid = ri[:, 0:2]
    rank = ri[:, 2:4]
    onehot = eid[:, :, None] == jnp.arange(ne, dtype=jnp.int32)[None, None, :]
    pos = jnp.sum(jnp.where(onehot, pad_off[None, None, :], 0), axis=-1) + rank
    code = _slot_codes(pos.astype(jnp.int32).reshape(-1))
    y = _gmm(t, code, tile_expert, n_used, layer, *expert_w)
    return _post(x, y, rw, pp, ps, vecs, emit_next)


def _mm_kernel(a_ref, w_ref, *out_refs, keep_tiles):
    acc = jnp.dot(a_ref[...], w_ref[...], preferred_element_type=F32)
    out_refs[0][...] = acc.astype(out_refs[0].dtype)
    if keep_tiles:
        i = pl.program_id(1)
        keep = functools.reduce(jnp.logical_or, [i == k for k in keep_tiles])

        @pl.when(keep)
        def _():
            out_refs[1][...] = acc


def _matmul(a, w, out_dtype, keep_tiles=(), name="matmul"):
    m, k = a.shape
    n_out = w.shape[1]
    tm = ROW_TILE
    tn = min(MM_TN, n_out)
    out_specs = [pl.BlockSpec((tm, tn), lambda j, i: (i, j))]
    out_shape = [jax.ShapeDtypeStruct((m, n_out), out_dtype)]
    if keep_tiles:
        def slot(j, i):
            return (sum((i > b).astype(jnp.int32) for b in keep_tiles), j)

        out_specs.append(pl.BlockSpec((tm, tn), slot))
        out_shape.append(jax.ShapeDtypeStruct((len(keep_tiles) * tm, n_out), F32))
    outs = pl.pallas_call(
        functools.partial(_mm_kernel, keep_tiles=tuple(keep_tiles)),
        grid=(n_out // tn, m // tm),
        in_specs=[
            pl.BlockSpec((tm, k), lambda j, i: (i, 0)),
            pl.BlockSpec((k, tn), lambda j, i: (0, j)),
        ],
        out_specs=out_specs,
        out_shape=out_shape,
        compiler_params=_params(),
        name=name,
    )(a, w)
    return outs


def _lane_half(shape, hh):
    return (lax.broadcasted_iota(jnp.int32, shape, 1) < HEAD_DIM) == (hh == 0)


def _attend_group(problems):
    scores, tops = [], []
    for q2, hh, pieces in problems:
        qm = jnp.where(_lane_half(q2.shape, hh), q2, jnp.zeros_like(q2))
        ss = [lax.dot_general(qm, k, (((1,), (1,)), ((), ())), preferred_element_type=F32) + bias
              for k, _, bias, _ in pieces]
        m = None
        for s, (_, _, _, pen) in zip(ss, pieces):
            rm = jnp.max(s, axis=-1, keepdims=True) + pen
            m = rm if m is None else jnp.maximum(m, rm)
        scores.append(ss)
        tops.append(m)
    outs = []
    for (q2, hh, pieces), ss, m in zip(problems, scores, tops):
        o = None
        for s, (_, v, _, pen) in zip(ss, pieces):
            vm = jnp.where(_lane_half(v.shape, hh), v, jnp.ones_like(v))
            po = jnp.dot(jnp.exp2(s - (m - pen)).astype(BF16), vm, preferred_element_type=F32)
            o = po if o is None else o + po
        outs.append(o)
    return outs


def _merge_heads(o_a, o_b):
    first = _lane_half(o_a.shape, 0)
    o = jnp.where(first, o_a, o_b)
    den = jnp.where(first, pltpu.roll(o_a, HEAD_DIM, 1), pltpu.roll(o_b, HEAD_DIM, 1))
    return o / den


def _attn_prompt_kernel(q_ref, k0_ref, k1_ref, k2_ref, v0_ref, v1_ref, v2_ref, bias_ref, o_ref):
    cg = pl.program_id(2)
    k_refs = (k0_ref, k1_ref, k2_ref)
    v_refs = (v0_ref, v1_ref, v2_ref)
    pens = (jnp.where(cg >= 2, 0.0, NEG_INF), jnp.where(cg >= 1, 0.0, NEG_INF), 0.0)
    qh = ATTN_Q // 2
    n_pairs = ATTN_LANES // (2 * HEAD_DIM)

    def cols(p):
        return slice(p * 2 * HEAD_DIM, (p + 1) * 2 * HEAD_DIM)

    def pieces(p, hh, half):
        out = []
        lo, hi = half * qh, half * qh + ATTN_KW
        for kb in range(3):
            a, b = max(lo, kb * ATTN_Q), min(hi, (kb + 1) * ATTN_Q)
            if a < b:
                rows = slice(a - kb * ATTN_Q, b - kb * ATTN_Q)
                out.append((k_refs[kb][rows, cols(p)], v_refs[kb][rows, cols(p)],
                            bias_ref[2 * p + hh, :, a - lo:b - lo], pens[kb]))
        return out

    for p0 in range(0, n_pairs, ATTN_GROUP_PAIRS):
        keys = [(p, hh, half) for p in range(p0, p0 + ATTN_GROUP_PAIRS) for hh in range(2) for half in range(2)]
        outs = _attend_group([(q_ref[half * qh:(half + 1) * qh, cols(p)], hh, pieces(p, hh, half))
                              for p, hh, half in keys])
        res = dict(zip(keys, outs))
        for p in range(p0, p0 + ATTN_GROUP_PAIRS):
            for half in range(2):
                o_ref[half * qh:(half + 1) * qh, cols(p)] = _merge_heads(
                    res[p, 0, half], res[p, 1, half]).astype(o_ref.dtype)


def _attn_prompt(q, k, v, bias):
    d = D_MODEL
    cgs = SEQ // ATTN_Q
    hpg = ATTN_LANES // HEAD_DIM

    def kmap(back):
        return lambda hg, b, c: (b * cgs + jnp.maximum(c - back, 0), hg)

    blk = (ATTN_Q, ATTN_LANES)
    return pl.pallas_call(
        _attn_prompt_kernel,
        grid=(d // ATTN_LANES, BATCH, cgs),
        in_specs=[
            pl.BlockSpec(blk, lambda hg, b, c: (b * cgs + c, hg)),
            pl.BlockSpec(blk, kmap(2)), pl.BlockSpec(blk, kmap(1)), pl.BlockSpec(blk, kmap(0)),
            pl.BlockSpec(blk, kmap(2)), pl.BlockSpec(blk, kmap(1)), pl.BlockSpec(blk, kmap(0)),
            pl.BlockSpec((hpg, ATTN_Q // 2, ATTN_KW), lambda hg, b, c: (hg, 0, 0)),
        ],
        out_specs=pl.BlockSpec(blk, lambda hg, b, c: (b * cgs + c, hg)),
        out_shape=jax.ShapeDtypeStruct((_n_tot(), d), BF16),
        compiler_params=_params(),
        name="attn_prompt",
    )(q, k, k, k, v, v, v, bias)


def _attn_sample_kernel(q_ref, kn_ref, vn_ref, ck_ref, cv_ref, bc_ref, bn_ref, o_in_ref, o_ref):
    del o_in_ref
    n_pairs = ATTN_LANES // (2 * HEAD_DIM)
    problems = []
    for p in range(n_pairs):
        cols = slice(p * 2 * HEAD_DIM, (p + 1) * 2 * HEAD_DIM)
        kc = ck_ref[0, :, cols].astype(BF16)
        vc = cv_ref[0, :, cols].astype(BF16)
        for hh in range(2):
            problems.append((q_ref[:, cols], hh, [(kc, vc, bc_ref[2 * p + hh], 0.0),
                                                  (kn_ref[:, cols], vn_ref[:, cols], bn_ref[2 * p + hh], 0.0)]))
    outs = _attend_group(problems)
    for p in range(n_pairs):
        cols = slice(p * 2 * HEAD_DIM, (p + 1) * 2 * HEAD_DIM)
        o_ref[:, cols] = _merge_heads(outs[2 * p], outs[2 * p + 1]).astype(o_ref.dtype)


def _attn_sample(q, k, v, ck, cv, bias_c, bias_n, o):
    d = D_MODEL
    off = _n_prompt() // DEC_SEQ
    hpg = ATTN_LANES // HEAD_DIM
    rows = ck.shape[1]
    new = pl.BlockSpec((DEC_SEQ, ATTN_LANES), lambda hg, b: (off + b, hg))
    cache = pl.BlockSpec((1, rows, ATTN_LANES), lambda hg, b: (b, 0, hg))
    return pl.pallas_call(
        _attn_sample_kernel,
        grid=(d // ATTN_LANES, DEC_BATCH),
        in_specs=[
            new, new, new, cache, cache,
            pl.BlockSpec((hpg, DEC_SEQ, rows), lambda hg, b: (hg, 0, 0)),
            pl.BlockSpec((hpg, DEC_SEQ, DEC_SEQ), lambda hg, b: (hg, 0, 0)),
            pl.BlockSpec(memory_space=pl.ANY),
        ],
        out_specs=new,
        out_shape=jax.ShapeDtypeStruct((_n_tot(), d), BF16),
        input_output_aliases={7: 0},
        compiler_params=_params(),
        name="attn_sample",
    )(q, k, v, ck, cv, bias_c, bias_n, o)


def _rel_index(dist):
    return jnp.clip(dist, -REL_CLIP, REL_CLIP) + REL_CLIP


def _attn_biases(rel_bias, cache_rows):
    tbl = rel_bias * LOG2_E
    n_heads = tbl.shape[0]
    nq = ATTN_Q // 2
    kw = ATTN_KW
    back = N_LEFT_CHUNKS * CHUNK
    span = kw + nq - 1
    u = tbl[:, _rel_index(back + (nq - 1) - jnp.arange(span))]
    rep = jnp.broadcast_to(jnp.pad(u, ((0, 0), (0, 1)))[:, None, :], (n_heads, nq, span + 1))
    toep = rep.reshape(n_heads, nq * (span + 1))[:, :nq * span].reshape(n_heads, nq, span)
    qi = jnp.arange(nq)[:, None]
    kj = jnp.arange(kw)[None, :]
    qc = qi // CHUNK
    kc = kj // CHUNK - back // CHUNK
    vis = (kc <= qc) & (kc >= qc - N_LEFT_CHUNKS)
    bias_p = jnp.where(vis[None], toep[:, :, nq - 1:], NEG_INF)
    si = jnp.arange(DEC_SEQ)[:, None]
    dist_c = si + cache_rows - jnp.arange(cache_rows)[None, :]
    dist_n = si - jnp.arange(DEC_SEQ)[None, :]
    return bias_p, tbl[:, _rel_index(dist_c)], tbl[:, _rel_index(dist_n)]


def _ln1_kernel(x_ref, m_ref, p_ref, vec_ref, wr_ref, br_ref, carry_ref, *rest, tm, aliased):
    if aliased:
        rest = rest[4:]
    x1_ref, t1_ref, ri_ref, rw_ref, cnt_out_ref, cnt = rest
    g1 = p_ref[0, 0:1, :]
    x1 = _layer_norm(_deepnorm_alpha() * x_ref[...] + (1.0 + g1) * m_ref[...], vec_ref[0:1, :], vec_ref[1:2, :])
    x1_ref[...] = x1
    t1 = x1 * (1.0 + p_ref[0, 2:3, :]) + p_ref[0, 1:2, :]
    t1_ref[...] = t1
    _route(t1, wr_ref, br_ref, carry_ref, ri_ref, rw_ref, cnt_out_ref, cnt, tm=tm)


def _ln1(x, m, pp, ps, vecs, wr, br):
    d = D_MODEL
    n_tot = _n_tot()
    tps = SEQ // ROW_TILE
    off = _n_prompt() // DEC_SEQ
    out_shape = [
        jax.ShapeDtypeStruct((n_tot, d), F32),
        jax.ShapeDtypeStruct((n_tot, d), F32),
        jax.ShapeDtypeStruct((n_tot, ROUTER_LANES), jnp.int32),
        jax.ShapeDtypeStruct((n_tot, ROUTER_LANES), F32),
        jax.ShapeDtypeStruct((1, ROUTER_LANES), F32),
    ]

    def call(tm, grid, roff, par, bmap, carry, prev, name):
        blk = pl.BlockSpec((tm, d), lambda i: (roff + i, 0))
        rblk = pl.BlockSpec((tm, ROUTER_LANES), lambda i: (roff + i, 0))
        one = pl.BlockSpec((1, ROUTER_LANES), lambda i: (0, 0))
        in_specs = [blk, blk, pl.BlockSpec((1, 3, d), lambda i: (bmap(i), 0, 0)), pl.BlockSpec((2, d), lambda i: (0, 0)),
                    pl.BlockSpec((d, 2 * ROUTER_LANES), lambda i: (0, 0)), one, one]
        args = [x, m, par, vecs, wr, br, carry]
        aliases = {}
        if prev is not None:
            in_specs += [pl.BlockSpec(memory_space=pl.ANY)] * 4
            args += list(prev)
            aliases = {7: 0, 8: 1, 9: 2, 10: 3}
        return pl.pallas_call(
            functools.partial(_ln1_kernel, tm=tm, aliased=prev is not None),
            grid=(grid,),
            in_specs=in_specs,
            out_specs=[blk, blk, rblk, rblk, one],
            out_shape=out_shape,
            scratch_shapes=[pltpu.VMEM((1, ROUTER_LANES), F32)],
            input_output_aliases=aliases,
            compiler_params=_params(),
            name=name,
        )(*args)

    x1, t1, ri, rw, cnt = call(ROW_TILE, _n_prompt() // ROW_TILE, 0, pp, lambda i: i // tps,
                               jnp.zeros((1, ROUTER_LANES), F32), None, "attn_post_prompt")
    x1, t1, ri, rw, cnt = call(DEC_SEQ, DEC_BATCH, off, ps, lambda i: i, cnt, (x1, t1, ri, rw), "attn_post_sample")
    return x1, t1, (ri, rw, cnt)


def _split_mods(mods_l):
    return jnp.split(mods_l, 6, axis=-1)


def _stack(vectors):
    st = jnp.stack(vectors, axis=1)
    return st[:BATCH], st[BATCH:BATCH + DEC_BATCH]


def _router_weights(rg_w, rg_b, re_w, re_b):
    d = D_MODEL
    ne = _n_experts()
    wr = jnp.zeros((d, ROUTER_LANES), F32)
    wr = wr.at[:, :N_GROUPS].set(rg_w)
    wr = wr.at[:, EXPERT_LANE0:EXPERT_LANE0 + ne].set(jnp.transpose(re_w, (1, 0, 2)).reshape(d, ne))
    br = jnp.zeros((1, ROUTER_LANES), F32)
    br = br.at[0, :N_GROUPS].set(rg_b)
    br = br.at[0, EXPERT_LANE0:EXPERT_LANE0 + ne].set(re_b.reshape(ne))
    wr_hi = wr.astype(BF16)
    wr_lo = (wr - wr_hi.astype(F32)).astype(BF16)
    return jnp.concatenate([wr_hi, wr_lo], axis=1), br


def kernel(x_prompt, x_sample, c_prompt, c_sample, state_pool, cache_k, cache_v, ada_w, ada_b, ln_g, ln_b,
           pool_w, pool_scale, attn_w_qkv, attn_w_o, attn_rel_bias, router_group_w, router_group_b,
           router_expert_w, router_expert_b, moe_w_gate, moe_w_up, moe_w_down):
    d = D_MODEL
    n_p = _n_prompt()
    xp = x_prompt.reshape(n_p, d)
    xs = x_sample.reshape(_n_sample(), d)

    n_c = BATCH + DEC_BATCH
    c_rows = 32
    c_all = jnp.concatenate([c_prompt, c_sample, jnp.zeros((c_rows - n_c, d), F32)], axis=0)
    mods = _ada_params(c_all, ada_w, ada_b)
    sh1_0, sc1_0, g1_0, sh2_0, sc2_0, g2_0 = _split_mods(mods[0])
    sh1_1, sc1_1, g1_1, sh2_1, sc2_1, g2_1 = _split_mods(mods[1])

    expert_w = (moe_w_gate.astype(BF16), moe_w_up.astype(BF16), moe_w_down.astype(BF16))

    def router_w(l):
        return _router_weights(router_group_w[l], router_group_b[l], router_expert_w[l], router_expert_b[l])

    hist_pad = jnp.pad(state_pool[0], ((0, 0), (HALO - POOL_STATE, 0), (0, 0))).reshape(DEC_BATCH * HALO, d)
    pp, ps = _stack([sh1_0, sc1_0, g1_0, sh2_0, sc2_0])
    vecs = jnp.stack([pool_scale[0], ln_g[0, 0], ln_b[0, 0]])
    x1, t1, st_p, h_s = _pool_layer(xp, xs, hist_pad, pp, ps, pool_w[0].astype(BF16), vecs)
    pp, ps = _stack([g2_0, sh1_1, sc1_1])
    x2, a2 = _moe(t1, x1, 0, _router(t1, *router_w(0)), expert_w, pp, ps,
                  jnp.stack([ln_g[0, 1], ln_b[0, 1]]), True)

    tiles_per_seq = SEQ // ROW_TILE
    keep_rows = min(N_LEFT_CHUNKS * CHUNK, SEQ)
    kt = keep_rows // ROW_TILE
    keep_tiles = [b * tiles_per_seq + tiles_per_seq - kt + r for b in range(BATCH) for r in range(kt)]
    keep_tiles.append(n_p // ROW_TILE)
    w_qkv = attn_w_qkv[0]
    w_q = (w_qkv[:, :d] * (HEAD_DIM ** -0.5 * LOG2_E)).astype(BF16)
    (q,) = _matmul(a2, w_q, BF16, name="attn_q")
    k, k_keep = _matmul(a2, w_qkv[:, d:2 * d].astype(BF16), BF16, keep_tiles, name="attn_k")
    v, v_keep = _matmul(a2, w_qkv[:, 2 * d:].astype(BF16), BF16, keep_tiles, name="attn_v")
    cache_rows = cache_k.shape[2]
    bias_p, bias_c, bias_n = _attn_biases(attn_rel_bias[0], cache_rows)
    o = _attn_prompt(q, k, v, bias_p)
    o = _attn_sample(q, k, v, cache_k[0].reshape(DEC_BATCH, cache_rows, d),
                     cache_v[0].reshape(DEC_BATCH, cache_rows, d), bias_c, bias_n, o)
    (m,) = _matmul(o, attn_w_o[0].astype(BF16), F32, name="attn_out")
    pp, ps = _stack([g1_1, sh2_1, sc2_1])
    x3, t3, routed = _ln1(x2, m, pp, ps, jnp.stack([ln_g[1, 0], ln_b[1, 0]]), *router_w(1))
    pp, ps = _stack([g2_1])
    yp, ys = _moe(t3, x3, 1, routed, expert_w, pp, ps, jnp.stack([ln_g[1, 1], ln_b[1, 1]]), False)

    n_heads = d // HEAD_DIM
    kp_rows = BATCH * keep_rows

    def prompt_cache(keep):
        return keep[:kp_rows].reshape(1, BATCH, keep_rows, n_heads, HEAD_DIM)

    def sample_cache(keep):
        return keep[kp_rows:kp_rows + _n_sample()].reshape(1, DEC_BATCH, DEC_SEQ, n_heads, HEAD_DIM)

    return (
        yp.reshape(BATCH, SEQ, d),
        ys.reshape(DEC_BATCH, DEC_SEQ, d),
        st_p[:, HALO - POOL_STATE:][None],
        prompt_cache(k_keep),
        prompt_cache(v_keep),
        h_s[:, HALO - POOL_STATE:][None],
        sample_cache(k_keep),
        sample_cache(v_keep),
    )
```
